```python
import jax
import jax.numpy as jnp
from jax import lax
import numpy as np


D_MODEL = 1024
BATCH = 16
SEQ = 2048
DEPTH = 2

CHUNK = 64
HEAD_DIM = 64
N_HEADS_A = 8
N_IDX_HEADS = 8
D_IDX = 64
TOPK_MAX = 256
Q_BLOCK = 128
N_HEADS_B = 8
N_PAST_CHUNKS = 8
BAND = (N_PAST_CHUNKS + 1) * CHUNK
REL_LO = -(CHUNK - 1)
REL_HI = 256
N_REL = REL_HI - REL_LO + 1
ROT_DIM = HEAD_DIM // 4
ROPE_THETA = 500000.0
D_FF = -(-8 * D_MODEL // (3 * 256)) * 256
EPS = 1e-6
W_A = N_HEADS_A * HEAD_DIM
W_B = N_HEADS_B * HEAD_DIM
INDEX_SCALE = (D_IDX ** -0.5) * (N_IDX_HEADS ** -0.5)
IN_SIZES = (W_A, HEAD_DIM, HEAD_DIM, N_IDX_HEADS * D_IDX, D_IDX, N_IDX_HEADS,
            W_B, W_B, W_B, D_MODEL, D_MODEL)
D_IN = sum(IN_SIZES)

kernel_name = 'chunk_causal_hybrid_dsa_band_attention_block'


def rms_norm(x, g):
    xf = x.astype(jnp.float32)
    y = xf * lax.rsqrt(jnp.mean(xf * xf, axis=-1, keepdims=True) + EPS)
    return (y * g.astype(jnp.float32)).astype(x.dtype)


def rope_tables(positions):
    inv_freq = ROPE_THETA ** (-jnp.arange(0, ROT_DIM, 2, dtype=jnp.float32) / ROT_DIM)
    ang = positions.astype(jnp.float32)[..., None] * inv_freq
    return jnp.cos(ang)[:, :, None, :], jnp.sin(ang)[:, :, None, :]


def partial_rope(x, cos, sin):
    half = ROT_DIM // 2
    xr = x[..., :ROT_DIM].astype(jnp.float32)
    x1, x2 = xr[..., :half], xr[..., half:]
    rot = jnp.concatenate([x1 * cos - x2 * sin, x2 * cos + x1 * sin], axis=-1)
    return jnp.concatenate([rot.astype(x.dtype), x[..., ROT_DIM:]], axis=-1)


def to_blocks(a, size):
    b, s = a.shape[:2]
    return a.reshape(b, s // size, size, *a.shape[2:]).swapaxes(0, 1)


def from_blocks(a):
    n, b, size = a.shape[:3]
    return a.swapaxes(0, 1).reshape(b, n * size, *a.shape[3:])


def split_cols(z):
    offs, acc = [], 0
    for w in IN_SIZES[:-1]:
        acc += w
        offs.append(acc)
    return jnp.split(z, offs, axis=-1)


def sparse_indexed_attention(q, k, v, q_idx, k_idx, w_idx):
    s_len = q.shape[1]
    top_k = min(TOPK_MAX, s_len // 4)
    key_pos = jnp.arange(s_len)

    def block(args):
        qb, qib, wb, bid = args
        t = bid * Q_BLOCK + jnp.arange(Q_BLOCK)
        limit = (t // CHUNK + 1) * CHUNK
        admissible = key_pos[None, :] < limit[:, None]
        logits = jnp.einsum('bqhd,bsd->bqhs', qib, k_idx).astype(jnp.float32)
        score = jnp.einsum('bqh,bqhs->bqs', wb.astype(jnp.float32), jax.nn.relu(logits)) * INDEX_SCALE
        score = jnp.where(admissible[None], score, -jnp.inf)
        _, sel = lax.top_k(score, top_k)
        valid = sel < limit[None, :, None]
        kg = jax.vmap(lambda kb, ib: kb[ib])(k, sel)
        vg = jax.vmap(lambda vb, ib: vb[ib])(v, sel)
        s = jnp.einsum('bqhd,bqkd->bqhk', qb, kg).astype(jnp.float32) * (HEAD_DIM ** -0.5)
        s = jnp.where(valid[:, :, None, :], s, -jnp.inf)
        p = jax.nn.softmax(s, axis=-1).astype(vg.dtype)
        return jnp.einsum('bqhk,bqkd->bqhd', p, vg)

    n_blk = s_len // Q_BLOCK
    out = lax.map(block, (to_blocks(q, Q_BLOCK), to_blocks(q_idx, Q_BLOCK),
                          to_blocks(w_idx, Q_BLOCK), jnp.arange(n_blk)))
    return from_blocks(out)


def chunked_band_attention(q, k, v, rel_bias):
    s_len = q.shape[1]
    pad = N_PAST_CHUNKS * CHUNK
    kp = jnp.pad(k, ((0, 0), (pad, 0), (0, 0), (0, 0)))
    vp = jnp.pad(v, ((0, 0), (pad, 0), (0, 0), (0, 0)))
    i = jnp.arange(CHUNK)[:, None]
    j = jnp.arange(BAND)
    rel = jnp.clip(i - j[None, :] + pad, REL_LO, REL_HI) - REL_LO
    bias = rel_bias[:, rel].astype(jnp.float32)

    def chunk(args):
        qc, c = args
        kb = lax.dynamic_slice_in_dim(kp, c * CHUNK, BAND, axis=1)
        vb = lax.dynamic_slice_in_dim(vp, c * CHUNK, BAND, axis=1)
        valid = j >= (N_PAST_CHUNKS - c) * CHUNK
        s = jnp.einsum('bqhd,bkhd->bhqk', qc, kb).astype(jnp.float32) * (HEAD_DIM ** -0.5) + bias[None]
        s = jnp.where(valid[None, None, None, :], s, -jnp.inf)
        p = jax.nn.softmax(s, axis=-1).astype(vb.dtype)
        return jnp.einsum('bhqk,bkhd->bqhd', p, vb)

    out = lax.map(chunk, (to_blocks(q, CHUNK), jnp.arange(s_len // CHUNK)))
    return from_blocks(out)


def setup_inputs(seed: int = 0) -> dict:
    key = jax.random.key(seed)
    ks = jax.random.split(key, 20)
    f32 = jnp.float32
    D = D_MODEL
    nrm = lambda k, shape, scale: jax.random.normal(k, shape, f32) * scale
    x = nrm(ks[0], (BATCH, SEQ, D), 1.0)
    c = nrm(ks[1], (BATCH, D), 1.0)
    offset = jax.random.randint(ks[2], (BATCH, 1), 0, 4096, dtype=jnp.int32)
    positions = (offset + jnp.arange(SEQ, dtype=jnp.int32)[None, :]).astype(jnp.int32)
    return {
        'x': x,
        'c': c,
        'positions': positions,
        'w_ada': nrm(ks[3], (DEPTH, D, 6 * D), 0.5 * D ** -0.5),
        'b_ada': nrm(ks[4], (DEPTH, 6 * D), 0.02),
        'g_mix': 1.0 + nrm(ks[5], (DEPTH, D), 0.02),
        'w_in': nrm(ks[6], (DEPTH, D, D_IN), D ** -0.5),
        'b_in': nrm(ks[7], (DEPTH, D_IN), 0.02),
        'qn_a': 1.0 + nrm(ks[8], (DEPTH, HEAD_DIM), 0.02),
        'kn_a': 1.0 + nrm(ks[9], (DEPTH, HEAD_DIM), 0.02),
        'qn_b': 1.0 + nrm(ks[10], (DEPTH, HEAD_DIM), 0.02),
        'kn_b': 1.0 + nrm(ks[11], (DEPTH, HEAD_DIM), 0.02),
        'rel_bias': nrm(ks[12], (DEPTH, N_HEADS_B, N_REL), 0.1),
        'w_oa': nrm(ks[13], (DEPTH, W_A, D), W_A ** -0.5),
        'w_ob': nrm(ks[14], (DEPTH, W_B, D), W_B ** -0.5),
        'w_out': nrm(ks[15], (DEPTH, D, D), D ** -0.5),
        'g_ffn': 1.0 + nrm(ks[16], (DEPTH, D), 0.02),
        'w_gu': nrm(ks[17], (DEPTH, D, 2 * D_FF), D ** -0.5),
        'w_down': nrm(ks[18], (DEPTH, D_FF, D), D_FF ** -0.5),
    }


def reference(x, c, positions, w_ada, b_ada, g_mix, w_in, b_in, qn_a, kn_a, qn_b, kn_b,
              rel_bias, w_oa, w_ob, w_out, g_ffn, w_gu, w_down):
    b, s_len, _ = x.shape
    cos, sin = rope_tables(positions)
    c_act = jax.nn.silu(c)
    for l in range(DEPTH):
        mod = c_act @ w_ada[l] + b_ada[l]
        sh_m, sc_m, gt_m, sh_f, sc_f, gt_f = jnp.split(mod, 6, axis=-1)

        h = rms_norm(x, g_mix[l]) * (1.0 + sc_m[:, None]) + sh_m[:, None]
        z = h @ w_in[l] + b_in[l]
        qa, ka, va, qi, ki, wi, qb, kb, vb, ga, gb = split_cols(z)

        qa = partial_rope(rms_norm(qa.reshape(b, s_len, N_HEADS_A, HEAD_DIM), qn_a[l]), cos, sin)
        ka = partial_rope(rms_norm(ka.reshape(b, s_len, 1, HEAD_DIM), kn_a[l]), cos, sin)[:, :, 0]
        qi = partial_rope(qi.reshape(b, s_len, N_IDX_HEADS, D_IDX), cos, sin)
        ki = partial_rope(ki.reshape(b, s_len, 1, D_IDX), cos, sin)[:, :, 0]
        y_a = sparse_indexed_attention(qa, ka, va, qi, ki, wi).reshape(b, s_len, W_A) @ w_oa[l]

        qb = rms_norm(qb.reshape(b, s_len, N_HEADS_B, HEAD_DIM), qn_b[l])
        kb = rms_norm(kb.reshape(b, s_len, N_HEADS_B, HEAD_DIM), kn_b[l])
        vb = vb.reshape(b, s_len, N_HEADS_B, HEAD_DIM)
        y_b = chunked_band_attention(qb, kb, vb, rel_bias[l]).reshape(b, s_len, W_B) @ w_ob[l]

        merged = jax.nn.sigmoid(ga) * y_a + jax.nn.sigmoid(gb) * y_b
        x = x + gt_m[:, None] * (merged @ w_out[l])

        h2 = rms_norm(x, g_ffn[l]) * (1.0 + sc_f[:, None]) + sh_f[:, None]
        gate, up = jnp.split(h2 @ w_gu[l], 2, axis=-1)
        x = x + gt_f[:, None] * ((jax.nn.silu(gate) * up) @ w_down[l])
    return x
```

```python
import functools

import jax
import jax.numpy as jnp
from jax import lax
from jax.experimental import pallas as pl
from jax.experimental.pallas import tpu as pltpu

F32 = jnp.float32
BF16 = jnp.bfloat16
I32 = jnp.int32

CHUNK = 64
CHUNK_SHIFT = CHUNK.bit_length() - 1
HEAD_DIM = 64
N_HEADS = 8
TOPK_MAX = 256
N_PAST_CHUNKS = 8
REL_LO = -(CHUNK - 1)
REL_HI = 256
ROT_DIM = HEAD_DIM // 4
ROPE_THETA = 500000.0
EPS = 1e-6
W_HEADS = N_HEADS * HEAD_DIM
INDEX_SCALE = (HEAD_DIM ** -0.5) * (N_HEADS ** -0.5)
ATTN_SCALE = HEAD_DIM ** -0.5

LANES_V7X = 128
SUBLANES_V7X = 8
VMEM_BYTES_V7X = 64 * 1024 * 1024

Q_BLOCK = 128
KEY_CHUNK = 256
SMALL_W = 256
ROW_TILE = 256
INT_MIN = -(2 ** 31)
NEG_BIG = -1e30


def _dot(a, b):
    return jnp.dot(a, b, preferred_element_type=F32)


def _dot_nt(a, b):
    return lax.dot_general(a, b, (((1,), (1,)), ((), ())), preferred_element_type=F32)


def _mod_kernel(c_ref, w_ref, b_ref, o_ref):
    c = c_ref[...]
    c_act = (c * jax.nn.sigmoid(c)).astype(BF16)
    o_ref[0] = _dot(c_act, w_ref[0].astype(BF16)) + b_ref[0]


def _modulation(c, w_ada, b_ada):
    depth, d, n = w_ada.shape
    b = c.shape[0]
    tn = n // 6
    return pl.pallas_call(
        _mod_kernel,
        grid=(depth, n // tn),
        in_specs=[
            pl.BlockSpec((b, d), lambda l, j: (0, 0)),
            pl.BlockSpec((1, d, tn), lambda l, j: (l, 0, j)),
            pl.BlockSpec((1, 1, tn), lambda l, j: (l, 0, j)),
        ],
        out_specs=pl.BlockSpec((1, b, tn), lambda l, j: (l, 0, j)),
        out_shape=jax.ShapeDtypeStruct((depth, b, n), F32),
        name="adaln_modulation",
    )(c, w_ada, b_ada.reshape(depth, 1, n))


def _group_mean_sq(v, gmat):
    sq = v * v
    hi = sq.astype(BF16)
    lo = (sq - hi.astype(F32)).astype(BF16)
    return _dot(hi, gmat) + _dot(lo, gmat)


def _rope_cols(v, tab_c, tab_s1, tab_s2):
    cols = []
    for c0 in range(0, v.shape[1], LANES_V7X):
        vc = v[:, c0:c0 + LANES_V7X]
        fwd = pltpu.roll(vc, LANES_V7X - ROT_DIM // 2, axis=1)
        bwd = pltpu.roll(vc, ROT_DIM // 2, axis=1)
        cols.append(vc * tab_c + fwd * tab_s1 + bwd * tab_s2)
    return cols[0] if len(cols) == 1 else jnp.concatenate(cols, axis=1)


def _inproj_kernel(x_ref, mod_ref, g_ref, w_ref, b_ref, gmat_ref, tc_ref, ts1_ref, ts2_ref,
                   gain_ref, sgain_ref,
                   qa_ref, qi_ref, qb_ref, kb_ref, vb_ref, ga_ref, gb_ref, sm_ref):
    d = x_ref.shape[1]
    x = x_ref[...]
    y = x * lax.rsqrt(jnp.mean(x * x, axis=-1, keepdims=True) + EPS) * g_ref[...]
    h = y * (1.0 + mod_ref[0, 1:2, :]) + mod_ref[0, 0:1, :]
    z = _dot(h.astype(BF16), w_ref[...]) + b_ref[...]

    gmat = gmat_ref[...]
    tab_c, tab_s1, tab_s2 = tc_ref[...], ts1_ref[...], ts2_ref[...]
    w = W_HEADS

    def head_norm(v, gain):
        return v * lax.rsqrt(_group_mean_sq(v, gmat) + EPS) * gain

    qa = head_norm(z[:, 0:w], gain_ref[0:1, :])
    qa_ref[...] = _rope_cols(qa, tab_c, tab_s1, tab_s2).astype(BF16)
    qi_ref[...] = _rope_cols(z[:, w:2 * w], tab_c, tab_s1, tab_s2).astype(BF16)
    qb_ref[...] = head_norm(z[:, 2 * w:3 * w], gain_ref[1:2, :]).astype(BF16)
    kb_ref[...] = head_norm(z[:, 3 * w:4 * w], gain_ref[2:3, :]).astype(BF16)
    vb_ref[...] = z[:, 4 * w:5 * w].astype(BF16)
    ga_ref[...] = jax.nn.sigmoid(z[:, 5 * w:5 * w + d]).astype(BF16)
    gb_ref[...] = jax.nn.sigmoid(z[:, 5 * w + d:5 * w + 2 * d]).astype(BF16)

    sm = z[:, 5 * w + 2 * d:5 * w + 2 * d + SMALL_W]
    lane = lax.broadcasted_iota(I32, (1, LANES_V7X), 1)
    first_head = lane < HEAD_DIM
    c0 = sm[:, 0:LANES_V7X]
    c1 = sm[:, LANES_V7X:2 * LANES_V7X]
    ms0 = _group_mean_sq(c0, gmat[0:LANES_V7X, 0:LANES_V7X])
    c0 = jnp.where(first_head, c0 * lax.rsqrt(ms0 + EPS) * sgain_ref[0:1, :], c0)
    c1 = c1 * sgain_ref[1:2, :]
    t_c = jnp.where(first_head, tab_c, 1.0)
    t_s1 = jnp.where(first_head, tab_s1, 0.0)
    t_s2 = jnp.where(first_head, tab_s2, 0.0)
    sm_ref[:, 0:LANES_V7X] = _rope_cols(c0, t_c, t_s1, t_s2)
    sm_ref[:, LANES_V7X:2 * LANES_V7X] = _rope_cols(c1, t_c, t_s1, t_s2)


def _const_spec(shape):
    nd = len(shape)
    return pl.BlockSpec(shape, lambda i: (0,) * nd, pipeline_mode=pl.Buffered(1))


def _in_projection(xf, mod_l, g_mix, w_big, b_big, gmat, tabs, gains, sgains, seq):
    t, d = xf.shape
    tm = ROW_TILE
    tiles_per_batch = seq // tm
    n_big = w_big.shape[1]
    row = lambda width: pl.BlockSpec((tm, width), lambda i: (i, 0))
    out_widths = [W_HEADS] * 5 + [d, d]
    return pl.pallas_call(
        _inproj_kernel,
        grid=(t // tm,),
        in_specs=[
            row(d),
            pl.BlockSpec((1, 6, d), lambda i: (i // tiles_per_batch, 0, 0)),
            _const_spec((1, d)),
            _const_spec((d, n_big)),
            _const_spec((1, n_big)),
            _const_spec((W_HEADS, W_HEADS)),
            row(LANES_V7X), row(LANES_V7X), row(LANES_V7X),
            _const_spec((3, W_HEADS)),
            _const_spec((2, LANES_V7X)),
        ],
        out_specs=[row(wd) for wd in out_widths] + [row(SMALL_W)],
        out_shape=[jax.ShapeDtypeStruct((t, wd), BF16) for wd in out_widths]
        + [jax.ShapeDtypeStruct((t, SMALL_W), F32)],
        compiler_params=pltpu.CompilerParams(
            dimension_semantics=("arbitrary",), vmem_limit_bytes=48 * 1024 * 1024),
        name="in_projection",
    )(xf, mod_l, g_mix, w_big, b_big, gmat, *tabs, gains, sgains)


def _sparse_kernel(qi_ref, qa_ref, wt_ref, ki_ref, ka_ref, vt_ref, o_ref,
                   qi_s, qa_s, keys_s, acc_s, *, top_k):
    seq = qi_ref.shape[1]
    n_qb = seq // Q_BLOCK
    kc_rows = KEY_CHUNK
    sub_groups = kc_rows // SUBLANES_V7X

    def q_block(qb, carry):
        r0 = pl.multiple_of(qb * Q_BLOCK, Q_BLOCK)
        qi = qi_ref[0, pl.ds(r0, Q_BLOCK), :]
        qa = qa_ref[0, pl.ds(r0, Q_BLOCK), :]
        for h in range(N_HEADS):
            qi_s[h * Q_BLOCK:(h + 1) * Q_BLOCK, :] = qi[:, h * HEAD_DIM:(h + 1) * HEAD_DIM]
            qa_s[h * Q_BLOCK:(h + 1) * Q_BLOCK, :] = qa[:, h * HEAD_DIM:(h + 1) * HEAD_DIM]
        w_idx = wt_ref[0, qb]
        n_chunks = (r0 + Q_BLOCK + kc_rows - 1) // kc_rows
        lane = lax.broadcasted_iota(I32, (1, Q_BLOCK), 1)
        limit = (((r0 + lane) >> CHUNK_SHIFT) + 1) * CHUNK

        def score_chunk(j, c):
            k0 = pl.multiple_of(j * kc_rows, kc_rows)
            logits = _dot_nt(ki_ref[0, pl.ds(k0, kc_rows), :], qi_s[...])
            score = jnp.zeros((kc_rows, Q_BLOCK), F32)
            for h in range(N_HEADS):
                score = score + jnp.maximum(logits[:, h * Q_BLOCK:(h + 1) * Q_BLOCK], 0.0) * w_idx[h:h + 1, :]
            score = jnp.where(score == 0.0, 0.0, score)
            bits = lax.bitcast_convert_type(score, I32)
            key = bits ^ ((bits >> 31) & 0x7FFFFFFF)
            pos = k0 + lax.broadcasted_iota(I32, (kc_rows, Q_BLOCK), 0)
            keys_s[pl.ds(k0, kc_rows), :] = jnp.where(pos < limit, key, INT_MIN)
            return c

        lax.fori_loop(0, n_chunks, score_chunk, 0)

        def count(pred_fn):
            def body(j, acc):
                k0 = pl.multiple_of(j * kc_rows, kc_rows)
                hit = jnp.where(pred_fn(keys_s[pl.ds(k0, kc_rows), :]), 1, 0).astype(I32)
                return acc + hit.reshape(sub_groups, SUBLANES_V7X, Q_BLOCK).sum(axis=0)
            acc = lax.fori_loop(0, n_chunks, body, jnp.zeros((SUBLANES_V7X, Q_BLOCK), I32))
            return acc.sum(axis=0, keepdims=True)

        def search_bit(i, state):
            thr, cnt_thr = state
            cand = thr + jnp.left_shift(jnp.int32(1), 31 - i)
            cand_b = jnp.broadcast_to(cand, (kc_rows, Q_BLOCK))
            cnt = count(lambda kk: kk >= cand_b)
            take = cnt >= top_k
            return jnp.where(take, cand, thr), jnp.where(take, cnt, cnt_thr)

        thr0 = jnp.full((1, Q_BLOCK), INT_MIN, I32)
        cnt0 = jnp.broadcast_to(n_chunks * kc_rows, (1, Q_BLOCK)).astype(I32)
        need_search = r0 + Q_BLOCK > top_k
        thr, cnt_thr = lax.fori_loop(0, jnp.where(need_search, 32, 0), search_bit, (thr0, cnt0))
        cnt_thr = jnp.where(need_search, cnt_thr, top_k)

        @pl.when(jnp.max(cnt_thr) > top_k)
        def _():
            thr_b = jnp.broadcast_to(thr, (kc_rows, Q_BLOCK))
            need = (top_k - count(lambda kk: kk > thr_b)).astype(F32)
            r_i = lax.broadcasted_iota(I32, (kc_rows, kc_rows), 0)
            c_i = lax.broadcasted_iota(I32, (kc_rows, kc_rows), 1)
            below = jnp.where(c_i < r_i, 1.0, 0.0).astype(BF16)

            def drop_chunk(j, seen):
                k0 = pl.multiple_of(j * kc_rows, kc_rows)
                kk = keys_s[pl.ds(k0, kc_rows), :]
                eq = kk == thr_b
                eq_f = jnp.where(eq, 1.0, 0.0)
                rank = _dot(below, eq_f.astype(BF16)) + seen
                keys_s[pl.ds(k0, kc_rows), :] = jnp.where(eq & (rank >= need), INT_MIN, kk)
                return seen + jnp.sum(eq_f, axis=0, keepdims=True)

            lax.fori_loop(0, n_chunks, drop_chunk, jnp.zeros((1, Q_BLOCK), F32))

        sel_thr = jnp.broadcast_to(jnp.maximum(thr, INT_MIN + 1), (kc_rows, Q_BLOCK))
        acc_s[...] = jnp.zeros_like(acc_s)

        def attend_chunk(j, state):
            m_run, l_run = state
            k0 = pl.multiple_of(j * kc_rows, kc_rows)
            s_all = _dot_nt(ka_ref[0, pl.ds(k0, kc_rows), :], qa_s[...])
            sel = keys_s[pl.ds(k0, kc_rows), :] >= sel_thr
            v_t = vt_ref[0, j]
            m_rows, l_rows = [], []
            for h in range(N_HEADS):
                s = jnp.where(sel, s_all[:, h * Q_BLOCK:(h + 1) * Q_BLOCK], -jnp.inf)
                m_old = m_run[h:h + 1, :]
                m_new = jnp.maximum(m_old, jnp.max(s, axis=0, keepdims=True))
                alpha = jnp.exp(m_old - m_new)
                p = jnp.exp(s - m_new)
                l_rows.append(alpha * l_run[h:h + 1, :] + jnp.sum(p, axis=0, keepdims=True))
                m_rows.append(m_new)
                acc_s[h] = alpha * acc_s[h] + _dot(v_t, p.astype(BF16))
            return jnp.concatenate(m_rows, axis=0), jnp.concatenate(l_rows, axis=0)

        m0 = jnp.full((N_HEADS, Q_BLOCK), NEG_BIG, F32)
        l0 = jnp.zeros((N_HEADS, Q_BLOCK), F32)
        _, l_fin = lax.fori_loop(0, n_chunks, attend_chunk, (m0, l0))
        outs = [(acc_s[h] / l_fin[h:h + 1, :]).T for h in range(N_HEADS)]
        o_ref[0, pl.ds(r0, Q_BLOCK), :] = jnp.concatenate(outs, axis=1).astype(BF16)
        return carry

    lax.fori_loop(0, n_qb, q_block, 0)


def _sparse_attention(qi, qa, w_t, ki, ka, v_t, top_k):
    b, seq, w = qi.shape
    batch3 = lambda s1, s2: pl.BlockSpec((1, s1, s2), lambda i: (i, 0, 0))
    batch4 = lambda s1, s2, s3: pl.BlockSpec((1, s1, s2, s3), lambda i: (i, 0, 0, 0))
    return pl.pallas_call(
        functools.partial(_sparse_kernel, top_k=top_k),
        grid=(b,),
        in_specs=[
            batch3(seq, w), batch3(seq, w),
            batch4(seq // Q_BLOCK, N_HEADS, Q_BLOCK),
            batch3(seq, HEAD_DIM), batch3(seq, HEAD_DIM),
            batch4(seq // KEY_CHUNK, HEAD_DIM, KEY_CHUNK),
        ],
        out_specs=batch3(seq, w),
        out_shape=jax.ShapeDtypeStruct((b, seq, w), BF16),
        scratch_shapes=[
            pltpu.VMEM((N_HEADS * Q_BLOCK, HEAD_DIM), BF16),
            pltpu.VMEM((N_HEADS * Q_BLOCK, HEAD_DIM), BF16),
            pltpu.VMEM((seq, Q_BLOCK), I32),
            pltpu.VMEM((N_HEADS, HEAD_DIM, Q_BLOCK), F32),
        ],
        compiler_params=pltpu.CompilerParams(
            dimension_semantics=("arbitrary",), vmem_limit_bytes=48 * 1024 * 1024),
        name="sparse_attention",
    )(qi, qa, w_t, ki, ka, v_t)


BAND_PIECES = N_PAST_CHUNKS * CHUNK // Q_BLOCK + 1


def _band_kernel(q_ref, k_ref, v_ref, bias_ref, o_ref):
    seq = q_ref.shape[1]
    n_qb = seq // Q_BLOCK

    def q_block(qb, carry):
        r0 = pl.multiple_of(qb * Q_BLOCK, Q_BLOCK)
        q = q_ref[0, pl.ds(r0, Q_BLOCK), :]
        k_pieces, v_pieces, pad_pen = [], [], []
        for w in range(BAND_PIECES):
            kb_idx = qb - (BAND_PIECES - 1) + w
            k0 = pl.multiple_of(jnp.maximum(kb_idx, 0) * Q_BLOCK, Q_BLOCK)
            k_pieces.append(k_ref[0, pl.ds(k0, Q_BLOCK), :])
            v_pieces.append(v_ref[0, pl.ds(k0, Q_BLOCK), :])
            pad_pen.append(jnp.where(kb_idx >= 0, 0.0, NEG_BIG))
        outs = []
        for h in range(N_HEADS):
            hs = slice(h * HEAD_DIM, (h + 1) * HEAD_DIM)
            s = jnp.concatenate(
                [_dot_nt(q[:, hs], k_pieces[w][:, hs]) + pad_pen[w] for w in range(BAND_PIECES)], axis=1)
            s = s + bias_ref[h]
            p = jnp.exp(s - jnp.max(s, axis=-1, keepdims=True))
            denom = jnp.sum(p, axis=-1, keepdims=True)
            pb = p.astype(BF16)
            o = _dot(pb[:, 0:Q_BLOCK], v_pieces[0][:, hs])
            for w in range(1, BAND_PIECES):
                o = o + _dot(pb[:, w * Q_BLOCK:(w + 1) * Q_BLOCK], v_pieces[w][:, hs])
            outs.append(o / denom)
        o_ref[0, pl.ds(r0, Q_BLOCK), :] = jnp.concatenate(outs, axis=1).astype(BF16)
        return carry

    lax.fori_loop(0, n_qb, q_block, 0)


def _band_attention(qb, kb, vb, bias):
    b, seq, w = qb.shape
    batch3 = pl.BlockSpec((1, seq, w), lambda i: (i, 0, 0))
    return pl.pallas_call(
        _band_kernel,
        grid=(b,),
        in_specs=[batch3, batch3, batch3,
                  pl.BlockSpec(bias.shape, lambda i: (0, 0, 0))],
        out_specs=batch3,
        out_shape=jax.ShapeDtypeStruct((b, seq, w), BF16),
        compiler_params=pltpu.CompilerParams(
            dimension_semantics=("arbitrary",), vmem_limit_bytes=48 * 1024 * 1024),
        name="band_attention",
    )(qb, kb, vb, bias)


def _band_bias(rel_bias_l):
    n_keys = BAND_PIECES * Q_BLOCK
    i = jnp.arange(Q_BLOCK)[:, None]
    j = jnp.arange(n_keys)[None, :]
    ci, cj = i // CHUNK, j // CHUNK
    in_band = (cj >= ci) & (cj <= ci + N_PAST_CHUNKS)
    rel = jnp.clip(i - j + N_PAST_CHUNKS * CHUNK, REL_LO, REL_HI) - REL_LO
    return jnp.where(in_band[None], rel_bias_l[:, rel].astype(F32), NEG_BIG)


def _out_ffn_kernel(x_ref, a_ref, b_ref, ga_ref, gb_ref, mod_ref, woa_ref, wob_ref, wout_ref,
                    g_ref, wgu_ref, wdown_ref, o_ref):
    d_ff = wdown_ref.shape[0]
    y_a = _dot(a_ref[...], woa_ref[...])
    y_b = _dot(b_ref[...], wob_ref[...])
    merged = ga_ref[...].astype(F32) * y_a + gb_ref[...].astype(F32) * y_b
    x1 = x_ref[...] + mod_ref[0, 2:3, :] * _dot(merged.astype(BF16), wout_ref[...])

    y = x1 * lax.rsqrt(jnp.mean(x1 * x1, axis=-1, keepdims=True) + EPS) * g_ref[...]
    h2 = y * (1.0 + mod_ref[0, 4:5, :]) + mod_ref[0, 3:4, :]
    gu = _dot(h2.astype(BF16), wgu_ref[...])
    gate, up = gu[:, 0:d_ff], gu[:, d_ff:2 * d_ff]
    act = gate * jax.nn.sigmoid(gate) * up
    o_ref[...] = x1 + mod_ref[0, 5:6, :] * _dot(act.astype(BF16), wdown_ref[...])


def _out_ffn(xf, attn_a, attn_b, ga, gb, mod_l, w_oa, w_ob, w_out, g_ffn, w_gu, w_down, seq):
    t, d = xf.shape
    tm = ROW_TILE
    tiles_per_batch = seq // tm
    row = lambda width: pl.BlockSpec((tm, width), lambda i: (i, 0))
    return pl.pallas_call(
        _out_ffn_kernel,
        grid=(t // tm,),
        in_specs=[
            row(d), row(W_HEADS), row(W_HEADS), row(d), row(d),
            pl.BlockSpec((1, 6, d), lambda i: (i // tiles_per_batch, 0, 0)),
            _const_spec(w_oa.shape), _const_spec(w_ob.shape), _const_spec(w_out.shape),
            _const_spec((1, d)), _const_spec(w_gu.shape), _const_spec(w_down.shape),
        ],
        out_specs=row(d),
        out_shape=jax.ShapeDtypeStruct((t, d), F32),
        compiler_params=pltpu.CompilerParams(
            dimension_semantics=("arbitrary",), vmem_limit_bytes=56 * 1024 * 1024),
        name="out_proj_ffn",
    )(xf, attn_a, attn_b, ga, gb, mod_l, w_oa, w_ob, w_out, g_ffn, w_gu, w_down)


def _rope_tables(positions):
    half = ROT_DIM // 2
    inv_freq = ROPE_THETA ** (-jnp.arange(0, ROT_DIM, 2, dtype=F32) / ROT_DIM)
    ang = positions.astype(F32).reshape(-1, 1) * inv_freq
    cos, sin = jnp.cos(ang), jnp.sin(ang)
    t = cos.shape[0]
    ones = jnp.ones((t, HEAD_DIM - ROT_DIM), F32)
    zeros = jnp.zeros((t, HEAD_DIM - half), F32)
    tab_c = jnp.concatenate([cos, cos, ones], axis=1)
    tab_s1 = jnp.concatenate([-sin, zeros], axis=1)
    tab_s2 = jnp.concatenate([jnp.zeros((t, half), F32), sin, zeros[:, half:]], axis=1)
    rep = LANES_V7X // HEAD_DIM
    return tuple(jnp.tile(tb, (1, rep)) for tb in (tab_c, tab_s1, tab_s2))


def _pack_in_projection(w_in_l, b_in_l, d):
    sizes = (W_HEADS, HEAD_DIM, HEAD_DIM, W_HEADS, HEAD_DIM, N_HEADS, W_HEADS, W_HEADS, W_HEADS, d, d)
    offs = [0]
    for s in sizes:
        offs.append(offs[-1] + s)
    seg = lambda a, i: a[..., offs[i]:offs[i + 1]]
    order = (0, 3, 6, 7, 8, 9, 10, 1, 2, 4, 5)
    pad = SMALL_W - (3 * HEAD_DIM + N_HEADS)
    w_big = jnp.concatenate([seg(w_in_l, i) for i in order] + [jnp.zeros((d, pad), w_in_l.dtype)], axis=1)
    b_big = jnp.concatenate([seg(b_in_l, i) for i in order] + [jnp.zeros((pad,), b_in_l.dtype)], axis=0)
    return w_big.astype(BF16), b_big.reshape(1, -1).astype(F32)


def kernel(x, c, positions, w_ada, b_ada, g_mix, w_in, b_in, qn_a, kn_a, qn_b, kn_b, rel_bias,
           w_oa, w_ob, w_out, g_ffn, w_gu, w_down):
    b, seq, d = x.shape
    depth = w_ada.shape[0]
    t = b * seq
    top_k = min(TOPK_MAX, seq // 4)
    assert seq % KEY_CHUNK == 0 and seq % ROW_TILE == 0 and d % LANES_V7X == 0

    mod = _modulation(c, w_ada, b_ada).reshape(depth, b, 6, d)
    tabs = _rope_tables(positions)
    group = jnp.arange(W_HEADS) // HEAD_DIM
    gmat = jnp.where(group[:, None] == group[None, :], 1.0 / HEAD_DIM, 0.0).astype(BF16)
    lane = jnp.arange(LANES_V7X)
    tile_heads = lambda g: jnp.tile(g.astype(F32), N_HEADS)

    xf = x.reshape(t, d)
    for l in range(depth):
        w_big, b_big = _pack_in_projection(w_in[l], b_in[l], d)
        gains = jnp.stack([tile_heads(qn_a[l]) * ATTN_SCALE, tile_heads(qn_b[l]) * ATTN_SCALE,
                           tile_heads(kn_b[l])])
        sgains = jnp.stack([
            jnp.concatenate([kn_a[l].astype(F32), jnp.ones((LANES_V7X - HEAD_DIM,), F32)]),
            jnp.where((lane >= HEAD_DIM) & (lane < HEAD_DIM + N_HEADS), INDEX_SCALE, 1.0).astype(F32)])
        qa, qi, qb, kb, vb, ga, gb, sm = _in_projection(
            xf, mod[l], g_mix[l].reshape(1, d), w_big, b_big, gmat, tabs, gains, sgains, seq)

        ka = sm[:, 0:HEAD_DIM].astype(BF16).reshape(b, seq, HEAD_DIM)
        va_t = sm[:, HEAD_DIM:2 * HEAD_DIM].astype(BF16).reshape(b, seq // KEY_CHUNK, KEY_CHUNK, HEAD_DIM)
        va_t = va_t.transpose(0, 1, 3, 2)
        ki = sm[:, 2 * HEAD_DIM:3 * HEAD_DIM].astype(BF16).reshape(b, seq, HEAD_DIM)
        w_t = sm[:, 3 * HEAD_DIM:3 * HEAD_DIM + N_HEADS].reshape(b, seq // Q_BLOCK, Q_BLOCK, N_HEADS)
        w_t = w_t.transpose(0, 1, 3, 2)

        to3 = lambda a: a.reshape(b, seq, W_HEADS)
        attn_a = _sparse_attention(to3(qi), to3(qa), w_t, ki, ka, va_t, top_k)
        attn_b = _band_attention(to3(qb), to3(kb), to3(vb), _band_bias(rel_bias[l]))

        xf = _out_ffn(xf, attn_a.reshape(t, W_HEADS), attn_b.reshape(t, W_HEADS), ga, gb, mod[l],
                      w_oa[l].astype(BF16), w_ob[l].astype(BF16), w_out[l].astype(BF16),
                      g_ffn[l].reshape(1, d), w_gu[l].astype(BF16), w_down[l].astype(BF16), seq)
    return xf.reshape(b, seq, d)
```

```python
import functools

import jax
import jax.numpy as jnp
from jax import lax
from jax.experimental import pallas as pl
from jax.experimental.pallas import tpu as pltpu

F32 = jnp.float32
BF16 = jnp.bfloat16
I32 = jnp.int32

CHUNK = 64
CHUNK_SHIFT = CHUNK.bit_length() - 1
HEAD_DIM = 64
N_HEADS = 8
TOPK_MAX = 256
N_PAST_CHUNKS = 8
REL_LO = -(CHUNK - 1)
REL_HI = 256
ROT_DIM = HEAD_DIM // 4
ROPE_THETA = 500000.0
EPS = 1e-6
W_HEADS = N_HEADS * HEAD_DIM
INDEX_SCALE = (HEAD_DIM ** -0.5) * (N_HEADS ** -0.5)
ATTN_SCALE = HEAD_DIM ** -0.5
LOG2_E = 1.4426950408889634
HEAD_PAIRS = N_HEADS // 2

LANES_V7X = 128
SUBLANES_V7X = 8
VMEM_BYTES_V7X = 64 * 1024 * 1024
VMEM_LIMIT = VMEM_BYTES_V7X * 3 // 4
VMEM_LIMIT_FFN = VMEM_BYTES_V7X * 7 // 8

Q_BLOCK = 128
KEY_CHUNK = 512
SMALL_W = 256
ROW_TILE = 256
INT_MIN = -(2 ** 31)
NEG_BIG = -1e30


def _dot(a, b):
    return jnp.dot(a, b, preferred_element_type=F32)


def _dot_nt(a, b):
    return lax.dot_general(a, b, (((1,), (1,)), ((), ())), preferred_element_type=F32)


def _mod_kernel(c_ref, w_ref, b_ref, o_ref):
    c = c_ref[...]
    c_act = (c * jax.nn.sigmoid(c)).astype(BF16)
    o_ref[0] = _dot(c_act, w_ref[0].astype(BF16)) + b_ref[0]


def _modulation(c, w_ada, b_ada):
    depth, d, n = w_ada.shape
    b = c.shape[0]
    tn = n // 6
    return pl.pallas_call(
        _mod_kernel,
        grid=(depth, n // tn),
        in_specs=[
            pl.BlockSpec((b, d), lambda l, j: (0, 0)),
            pl.BlockSpec((1, d, tn), lambda l, j: (l, 0, j)),
            pl.BlockSpec((1, 1, tn), lambda l, j: (l, 0, j)),
        ],
        out_specs=pl.BlockSpec((1, b, tn), lambda l, j: (l, 0, j)),
        out_shape=jax.ShapeDtypeStruct((depth, b, n), F32),
        name="adaln_modulation",
    )(c, w_ada, b_ada.reshape(depth, 1, n))


def _group_mean_sq(v, gmat):
    sq = v * v
    hi = sq.astype(BF16)
    lo = (sq - hi.astype(F32)).astype(BF16)
    return _dot(hi, gmat) + _dot(lo, gmat)


def _rope_cols(v, tab_c, tab_s1, tab_s2):
    cols = []
    for c0 in range(0, v.shape[1], LANES_V7X):
        vc = v[:, c0:c0 + LANES_V7X]
        fwd = pltpu.roll(vc, LANES_V7X - ROT_DIM // 2, axis=1)
        bwd = pltpu.roll(vc, ROT_DIM // 2, axis=1)
        cols.append(vc * tab_c + fwd * tab_s1 + bwd * tab_s2)
    return cols[0] if len(cols) == 1 else jnp.concatenate(cols, axis=1)


def _inproj_kernel(x_ref, mod_ref, g_ref, w_ref, b_ref, gmat_ref, tc_ref, ts1_ref, ts2_ref,
                   gain_ref, sgain_ref,
                   qa_ref, qi_ref, qb_ref, kb_ref, vb_ref, ga_ref, gb_ref, sm_ref):
    d = x_ref.shape[1]
    x = x_ref[...]
    y = x * lax.rsqrt(jnp.mean(x * x, axis=-1, keepdims=True) + EPS) * g_ref[...]
    h = y * (1.0 + mod_ref[0, 1:2, :]) + mod_ref[0, 0:1, :]
    z = _dot(h.astype(BF16), w_ref[...]) + b_ref[...]

    gmat = gmat_ref[...]
    tab_c, tab_s1, tab_s2 = tc_ref[...], ts1_ref[...], ts2_ref[...]
    w = W_HEADS

    def head_norm(v, gain):
        return v * lax.rsqrt(_group_mean_sq(v, gmat) + EPS) * gain

    qa = head_norm(z[:, 0:w], gain_ref[0:1, :])
    qa_ref[...] = _rope_cols(qa, tab_c, tab_s1, tab_s2).astype(BF16)
    qi_ref[...] = _rope_cols(z[:, w:2 * w], tab_c, tab_s1, tab_s2).astype(BF16)
    qb = head_norm(z[:, 2 * w:3 * w], gain_ref[1:2, :]).astype(BF16)
    kb = head_norm(z[:, 3 * w:4 * w], gain_ref[2:3, :]).astype(BF16)
    vb = z[:, 4 * w:5 * w].astype(BF16)
    for pr in range(HEAD_PAIRS):
        cols = slice(pr * LANES_V7X, (pr + 1) * LANES_V7X)
        qb_ref[pr] = qb[:, cols]
        kb_ref[pr] = kb[:, cols]
        vb_ref[pr] = vb[:, cols]
    ga_ref[...] = jax.nn.sigmoid(z[:, 5 * w:5 * w + d]).astype(BF16)
    gb_ref[...] = jax.nn.sigmoid(z[:, 5 * w + d:5 * w + 2 * d]).astype(BF16)

    sm = z[:, 5 * w + 2 * d:5 * w + 2 * d + SMALL_W]
    lane = lax.broadcasted_iota(I32, (1, LANES_V7X), 1)
    first_head = lane < HEAD_DIM
    c0 = sm[:, 0:LANES_V7X]
    c1 = sm[:, LANES_V7X:2 * LANES_V7X]
    ms0 = _group_mean_sq(c0, gmat[0:LANES_V7X, 0:LANES_V7X])
    c0 = jnp.where(first_head, c0 * lax.rsqrt(ms0 + EPS) * sgain_ref[0:1, :], c0)
    c1 = c1 * sgain_ref[1:2, :]
    t_c = jnp.where(first_head, tab_c, 1.0)
    t_s1 = jnp.where(first_head, tab_s1, 0.0)
    t_s2 = jnp.where(first_head, tab_s2, 0.0)
    sm_ref[:, 0:LANES_V7X] = _rope_cols(c0, t_c, t_s1, t_s2)
    sm_ref[:, LANES_V7X:2 * LANES_V7X] = _rope_cols(c1, t_c, t_s1, t_s2)


def _const_spec(shape):
    nd = len(shape)
    return pl.BlockSpec(shape, lambda i: (0,) * nd, pipeline_mode=pl.Buffered(1))


def _in_projection(xf, mod_l, g_mix, w_big, b_big, gmat, tabs, gains, sgains, seq):
    t, d = xf.shape
    tm = ROW_TILE
    tiles_per_batch = seq // tm
    n_big = w_big.shape[1]
    row = lambda width: pl.BlockSpec((tm, width), lambda i: (i, 0))
    pair_major = pl.BlockSpec((HEAD_PAIRS, tm, LANES_V7X), lambda i: (0, i, 0))
    pm_shape = jax.ShapeDtypeStruct((HEAD_PAIRS, t, LANES_V7X), BF16)
    return pl.pallas_call(
        _inproj_kernel,
        grid=(t // tm,),
        in_specs=[
            row(d),
            pl.BlockSpec((1, 6, d), lambda i: (i // tiles_per_batch, 0, 0)),
            _const_spec((1, d)),
            _const_spec((d, n_big)),
            _const_spec((1, n_big)),
            _const_spec((W_HEADS, W_HEADS)),
            row(LANES_V7X), row(LANES_V7X), row(LANES_V7X),
            _const_spec((3, W_HEADS)),
            _const_spec((2, LANES_V7X)),
        ],
        out_specs=[row(W_HEADS), row(W_HEADS), pair_major, pair_major, pair_major, row(d), row(d), row(SMALL_W)],
        out_shape=[jax.ShapeDtypeStruct((t, W_HEADS), BF16)] * 2 + [pm_shape] * 3
        + [jax.ShapeDtypeStruct((t, d), BF16)] * 2 + [jax.ShapeDtypeStruct((t, SMALL_W), F32)],
        compiler_params=pltpu.CompilerParams(
            dimension_semantics=("arbitrary",), vmem_limit_bytes=VMEM_LIMIT),
        name="in_projection",
    )(xf, mod_l, g_mix, w_big, b_big, gmat, *tabs, gains, sgains)


def _sparse_kernel(qi_ref, qa_ref, wt_ref, ki_ref, ka_ref, vt_ref, o_ref,
                   qi_s, qa_s, keys_s, s_scr, acc_s, *, top_k):
    seq = qi_ref.shape[1]
    n_qb = seq // Q_BLOCK
    kc_rows = KEY_CHUNK
    sub_groups = kc_rows // SUBLANES_V7X

    def q_block(qb, carry):
        r0 = pl.multiple_of(qb * Q_BLOCK, Q_BLOCK)
        qi = qi_ref[0, pl.ds(r0, Q_BLOCK), :]
        qa = qa_ref[0, pl.ds(r0, Q_BLOCK), :]
        for h in range(N_HEADS):
            qi_s[h * Q_BLOCK:(h + 1) * Q_BLOCK, :] = qi[:, h * HEAD_DIM:(h + 1) * HEAD_DIM]
            qa_s[h * Q_BLOCK:(h + 1) * Q_BLOCK, :] = qa[:, h * HEAD_DIM:(h + 1) * HEAD_DIM]
        w_idx = wt_ref[0, qb]
        n_chunks = (r0 + Q_BLOCK + kc_rows - 1) // kc_rows
        lane = lax.broadcasted_iota(I32, (1, Q_BLOCK), 1)
        limit = (((r0 + lane) >> CHUNK_SHIFT) + 1) * CHUNK

        def score_chunk(j, c):
            k0 = pl.multiple_of(j * kc_rows, kc_rows)
            logits = _dot_nt(ki_ref[0, pl.ds(k0, kc_rows), :], qi_s[...])
            score = jnp.zeros((kc_rows, Q_BLOCK), F32)
            for h in range(N_HEADS):
                score = score + jnp.maximum(logits[:, h * Q_BLOCK:(h + 1) * Q_BLOCK], 0.0) * w_idx[h:h + 1, :]
            score = jnp.where(score == 0.0, 0.0, score)
            bits = lax.bitcast_convert_type(score, I32)
            key = bits ^ ((bits >> 31) & 0x7FFFFFFF)
            pos = k0 + lax.broadcasted_iota(I32, (kc_rows, Q_BLOCK), 0)
            keys_s[pl.ds(k0, kc_rows), :] = jnp.where(pos < limit, key, INT_MIN)
            return c

        lax.fori_loop(0, n_chunks, score_chunk, 0)

        def count(pred_fn):
            def body(j, acc):
                k0 = pl.multiple_of(j * kc_rows, kc_rows)
                hit = jnp.where(pred_fn(keys_s[pl.ds(k0, kc_rows), :]), 1, 0).astype(I32)
                return acc + hit.reshape(sub_groups, SUBLANES_V7X, Q_BLOCK).sum(axis=0)
            acc = lax.fori_loop(0, n_chunks, body, jnp.zeros((SUBLANES_V7X, Q_BLOCK), I32))
            return acc.sum(axis=0, keepdims=True)

        def search_bit(i, state):
            thr, cnt_thr = state
            cand = thr + jnp.left_shift(jnp.int32(1), 31 - i)
            cand_b = jnp.broadcast_to(cand, (kc_rows, Q_BLOCK))
            cnt = count(lambda kk: kk >= cand_b)
            take = cnt >= top_k
            return jnp.where(take, cand, thr), jnp.where(take, cnt, cnt_thr)

        thr0 = jnp.full((1, Q_BLOCK), INT_MIN, I32)
        cnt0 = jnp.broadcast_to(n_chunks * kc_rows, (1, Q_BLOCK)).astype(I32)
        need_search = r0 + Q_BLOCK > top_k
        thr, cnt_thr = lax.fori_loop(0, jnp.where(need_search, 32, 0), search_bit, (thr0, cnt0))
        cnt_thr = jnp.where(need_search, cnt_thr, top_k)

        @pl.when(jnp.max(cnt_thr) > top_k)
        def _():
            thr_b = jnp.broadcast_to(thr, (kc_rows, Q_BLOCK))
            need = (top_k - count(lambda kk: kk > thr_b)).astype(F32)
            r_i = lax.broadcasted_iota(I32, (kc_rows, kc_rows), 0)
            c_i = lax.broadcasted_iota(I32, (kc_rows, kc_rows), 1)
            below = jnp.where(c_i < r_i, 1.0, 0.0).astype(BF16)

            def drop_chunk(j, seen):
                k0 = pl.multiple_of(j * kc_rows, kc_rows)
                kk = keys_s[pl.ds(k0, kc_rows), :]
                eq = kk == thr_b
                eq_f = jnp.where(eq, 1.0, 0.0)
                rank = _dot(below, eq_f.astype(BF16)) + seen
                keys_s[pl.ds(k0, kc_rows), :] = jnp.where(eq & (rank >= need), INT_MIN, kk)
                return seen + jnp.sum(eq_f, axis=0, keepdims=True)

            lax.fori_loop(0, n_chunks, drop_chunk, jnp.zeros((1, Q_BLOCK), F32))

        sel_thr = jnp.broadcast_to(jnp.maximum(thr, INT_MIN + 1), (kc_rows, Q_BLOCK))
        wide = N_HEADS * Q_BLOCK

        def masked_logits(j, m_acc):
            k0 = pl.multiple_of(j * kc_rows, kc_rows)
            s_all = _dot_nt(ka_ref[0, pl.ds(k0, kc_rows), :], qa_s[...])
            sel = keys_s[pl.ds(k0, kc_rows), :] >= sel_thr
            s = jnp.concatenate(
                [jnp.where(sel, s_all[:, h * Q_BLOCK:(h + 1) * Q_BLOCK], -jnp.inf) for h in range(N_HEADS)],
                axis=1)
            s_scr[pl.ds(k0, kc_rows), :] = s
            return jnp.maximum(m_acc, s.reshape(sub_groups, SUBLANES_V7X, wide).max(axis=0))

        m_acc = lax.fori_loop(0, n_chunks, masked_logits,
                              jnp.full((SUBLANES_V7X, wide), -jnp.inf, F32))
        m_fin = jnp.max(m_acc, axis=0, keepdims=True)
        acc_s[...] = jnp.zeros_like(acc_s)

        def weigh_values(j, l_acc):
            k0 = pl.multiple_of(j * kc_rows, kc_rows)
            p = jnp.exp2(s_scr[pl.ds(k0, kc_rows), :] - m_fin)
            acc_s[...] += _dot(vt_ref[0, j], p.astype(BF16))
            return l_acc + p.reshape(sub_groups, SUBLANES_V7X, wide).sum(axis=0)

        l_acc = lax.fori_loop(0, n_chunks, weigh_values, jnp.zeros((SUBLANES_V7X, wide), F32))
        o_t = acc_s[...] / jnp.sum(l_acc, axis=0, keepdims=True)
        outs = [o_t[:, h * Q_BLOCK:(h + 1) * Q_BLOCK].T for h in range(N_HEADS)]
        o_ref[0, pl.ds(r0, Q_BLOCK), :] = jnp.concatenate(outs, axis=1).astype(BF16)
        return carry

    lax.fori_loop(0, n_qb, q_block, 0)


def _sparse_attention(qi, qa, w_t, ki, ka, v_t, top_k):
    b, seq, w = qi.shape
    batch3 = lambda s1, s2: pl.BlockSpec((1, s1, s2), lambda i: (i, 0, 0))
    batch4 = lambda s1, s2, s3: pl.BlockSpec((1, s1, s2, s3), lambda i: (i, 0, 0, 0))
    return pl.pallas_call(
        functools.partial(_sparse_kernel, top_k=top_k),
        grid=(b,),
        in_specs=[
            batch3(seq, w), batch3(seq, w),
            batch4(seq // Q_BLOCK, N_HEADS, Q_BLOCK),
            batch3(seq, HEAD_DIM), batch3(seq, HEAD_DIM),
            batch4(seq // KEY_CHUNK, HEAD_DIM, KEY_CHUNK),
        ],
        out_specs=batch3(seq, w),
        out_shape=jax.ShapeDtypeStruct((b, seq, w), BF16),
        scratch_shapes=[
            pltpu.VMEM((N_HEADS * Q_BLOCK, HEAD_DIM), BF16),
            pltpu.VMEM((N_HEADS * Q_BLOCK, HEAD_DIM), BF16),
            pltpu.VMEM((seq, Q_BLOCK), I32),
            pltpu.VMEM((seq, N_HEADS * Q_BLOCK), F32),
            pltpu.VMEM((HEAD_DIM, N_HEADS * Q_BLOCK), F32),
        ],
        compiler_params=pltpu.CompilerParams(
            dimension_semantics=("arbitrary",), vmem_limit_bytes=VMEM_LIMIT),
        name="sparse_attention",
    )(qi, qa, w_t, ki, ka, v_t)


BAND_Q = 256
BAND_KEYS = N_PAST_CHUNKS * CHUNK + BAND_Q
BAND_PIECES = BAND_KEYS // BAND_Q


def _band_kernel(q_ref, k_ref, vt_ref, bias_ref, o_ref, s_a, s_b):
    seq = q_ref.shape[2]
    n_items = (seq // BAND_Q) * HEAD_PAIRS
    groups = BAND_Q // SUBLANES_V7X
    lane = lax.broadcasted_iota(I32, (1, LANES_V7X), 1)
    head_lanes = (lane < HEAD_DIM, lane >= HEAD_DIM)

    def pieces_of(qb):
        out = []
        for w in range(BAND_PIECES):
            kb_idx = qb - (BAND_PIECES - 1) + w
            kb_c = jnp.maximum(kb_idx, 0)
            out.append((kb_c, pl.multiple_of(kb_c * BAND_Q, BAND_Q),
                        jnp.where(kb_idx >= 0, 0.0, NEG_BIG)))
        return out

    def logits(item, s_scr):
        qb, pr = item // HEAD_PAIRS, item % HEAD_PAIRS
        r0 = pl.multiple_of(qb * BAND_Q, BAND_Q)
        q2 = q_ref[pr, 0, pl.ds(r0, BAND_Q), :]
        for e in range(2):
            q_h = jnp.where(head_lanes[e], q2, jnp.zeros_like(q2))
            for w, (_, row, _) in enumerate(pieces_of(qb)):
                rows = slice(w * BAND_Q, (w + 1) * BAND_Q)
                k2 = k_ref[pr, 0, pl.ds(row, BAND_Q), :]
                s_scr[e, rows, :] = _dot_nt(k2, q_h) + bias_ref[2 * pr + e, rows, :]

    def softmax_values(item, s_scr):
        qb, pr = item // HEAD_PAIRS, item % HEAD_PAIRS
        r0 = pl.multiple_of(qb * BAND_Q, BAND_Q)
        pieces = pieces_of(qb)
        o_pair = []
        for e in range(2):
            m_acc = jnp.full((SUBLANES_V7X, BAND_Q), -jnp.inf, F32)
            for w, (_, _, pen) in enumerate(pieces):
                rows = slice(w * BAND_Q, (w + 1) * BAND_Q)
                m_acc = jnp.maximum(
                    m_acc, s_scr[e, rows, :].reshape(groups, SUBLANES_V7X, BAND_Q).max(axis=0) + pen)
            m_fin = jnp.max(m_acc, axis=0, keepdims=True)
            l_acc = jnp.zeros((SUBLANES_V7X, BAND_Q), F32)
            o_t = jnp.zeros((HEAD_DIM, BAND_Q), F32)
            for w, (kb_c, _, pen) in enumerate(pieces):
                rows = slice(w * BAND_Q, (w + 1) * BAND_Q)
                p = jnp.exp2(s_scr[e, rows, :] - (m_fin - pen))
                l_acc = l_acc + p.reshape(groups, SUBLANES_V7X, BAND_Q).sum(axis=0)
                v_h = vt_ref[0, pr, kb_c][e * HEAD_DIM:(e + 1) * HEAD_DIM, :]
                o_t = o_t + _dot(v_h, p.astype(BF16))
            o_pair.append(o_t / jnp.sum(l_acc, axis=0, keepdims=True))
        o_ref[pr, 0, pl.ds(r0, BAND_Q), :] = jnp.concatenate(o_pair, axis=0).T.astype(BF16)

    logits(0, s_a)

    def step(u, c):
        logits(2 * u + 1, s_b)
        softmax_values(2 * u, s_a)
        logits(2 * u + 2, s_a)
        softmax_values(2 * u + 1, s_b)
        return c

    lax.fori_loop(0, n_items // 2 - 1, step, 0)
    logits(n_items - 1, s_b)
    softmax_values(n_items - 2, s_a)
    softmax_values(n_items - 1, s_b)


def _band_attention(qb_pm, kb_pm, vb_t, bias_t):
    pairs, b, seq, lanes = qb_pm.shape
    batch4 = pl.BlockSpec((pairs, 1, seq, lanes), lambda i: (0, i, 0, 0))
    return pl.pallas_call(
        _band_kernel,
        grid=(b,),
        in_specs=[batch4, batch4,
                  pl.BlockSpec((1,) + vb_t.shape[1:], lambda i: (i, 0, 0, 0, 0)),
                  _const_spec(bias_t.shape)],
        out_specs=batch4,
        out_shape=jax.ShapeDtypeStruct(qb_pm.shape, BF16),
        scratch_shapes=[pltpu.VMEM((2, BAND_KEYS, BAND_Q), F32)] * 2,
        compiler_params=pltpu.CompilerParams(
            dimension_semantics=("arbitrary",), vmem_limit_bytes=VMEM_LIMIT),
        name="band_attention",
    )(qb_pm, kb_pm, vb_t, bias_t)


def _band_bias_t(rel_bias_l):
    n_rel = REL_HI - REL_LO + 1
    period = BAND_KEYS + BAND_Q
    pad = N_PAST_CHUNKS * CHUNK
    lo_pos = period - pad + REL_LO
    rb = rel_bias_l.astype(F32)
    hi_val, lo_val = rb[:, n_rel - 1:n_rel], rb[:, 0:1]
    vec = jnp.concatenate([
        jnp.broadcast_to(hi_val, (N_HEADS, BAND_Q + 1)),
        jnp.broadcast_to(lo_val, (N_HEADS, lo_pos - (BAND_Q + 1))),
        rb,
        jnp.broadcast_to(hi_val, (N_HEADS, period - lo_pos - n_rel)),
    ], axis=1)
    assert vec.shape[1] == period
    flat = jnp.tile(vec, (1, BAND_KEYS))[:, :BAND_KEYS * (period - 1)]
    table = flat.reshape(N_HEADS, BAND_KEYS, period - 1)[:, :, :BAND_Q]
    j = jnp.arange(BAND_KEYS)[:, None] // CHUNK
    i = jnp.arange(BAND_Q)[None, :] // CHUNK
    in_band = (j >= i) & (j <= i + N_PAST_CHUNKS)
    return jnp.where(in_band[None], table * LOG2_E, NEG_BIG)


def _out_ffn_kernel(x_ref, a_ref, b_ref, ga_ref, gb_ref, mod_ref, woa_ref, wob_ref, wout_ref,
                    g_ref, wgu_ref, wdown_ref, o_ref):
    d_ff = wdown_ref.shape[0]
    y_a = _dot(a_ref[...], woa_ref[...])
    attn_b = jnp.concatenate([b_ref[pr] for pr in range(HEAD_PAIRS)], axis=1)
    y_b = _dot(attn_b, wob_ref[...])
    merged = ga_ref[...].astype(F32) * y_a + gb_ref[...].astype(F32) * y_b
    x1 = x_ref[...] + mod_ref[0, 2:3, :] * _dot(merged.astype(BF16), wout_ref[...])

    y = x1 * lax.rsqrt(jnp.mean(x1 * x1, axis=-1, keepdims=True) + EPS) * g_ref[...]
    h2 = y * (1.0 + mod_ref[0, 4:5, :]) + mod_ref[0, 3:4, :]
    gu = _dot(h2.astype(BF16), wgu_ref[...])
    gate, up = gu[:, 0:d_ff], gu[:, d_ff:2 * d_ff]
    act = gate * jax.nn.sigmoid(gate) * up
    o_ref[...] = x1 + mod_ref[0, 5:6, :] * _dot(act.astype(BF16), wdown_ref[...])


def _out_ffn(xf, attn_a, attn_b, ga, gb, mod_l, w_oa, w_ob, w_out, g_ffn, w_gu, w_down, seq):
    t, d = xf.shape
    tm = ROW_TILE
    tiles_per_batch = seq // tm
    row = lambda width: pl.BlockSpec((tm, width), lambda i: (i, 0))
    return pl.pallas_call(
        _out_ffn_kernel,
        grid=(t // tm,),
        in_specs=[
            row(d), row(W_HEADS), pl.BlockSpec((HEAD_PAIRS, tm, LANES_V7X), lambda i: (0, i, 0)), row(d), row(d),
            pl.BlockSpec((1, 6, d), lambda i: (i // tiles_per_batch, 0, 0)),
            _const_spec(w_oa.shape), _const_spec(w_ob.shape), _const_spec(w_out.shape),
            _const_spec((1, d)), _const_spec(w_gu.shape), _const_spec(w_down.shape),
        ],
        out_specs=row(d),
        out_shape=jax.ShapeDtypeStruct((t, d), F32),
        compiler_params=pltpu.CompilerParams(
            dimension_semantics=("arbitrary",), vmem_limit_bytes=VMEM_LIMIT_FFN),
        name="out_proj_ffn",
    )(xf, attn_a, attn_b, ga, gb, mod_l, w_oa, w_ob, w_out, g_ffn, w_gu, w_down)


def _rope_tables(positions):
    half = ROT_DIM // 2
    inv_freq = ROPE_THETA ** (-jnp.arange(0, ROT_DIM, 2, dtype=F32) / ROT_DIM)
    ang = positions.astype(F32).reshape(-1, 1) * inv_freq
    cos, sin = jnp.cos(ang), jnp.sin(ang)
    t = cos.shape[0]
    ones = jnp.ones((t, HEAD_DIM - ROT_DIM), F32)
    zeros = jnp.zeros((t, HEAD_DIM - half), F32)
    tab_c = jnp.concatenate([cos, cos, ones], axis=1)
    tab_s1 = jnp.concatenate([-sin, zeros], axis=1)
    tab_s2 = jnp.concatenate([jnp.zeros((t, half), F32), sin, zeros[:, half:]], axis=1)
    rep = LANES_V7X // HEAD_DIM
    return tuple(jnp.tile(tb, (1, rep)) for tb in (tab_c, tab_s1, tab_s2))


def _pack_in_projection(w_in_l, b_in_l, d):
    sizes = (W_HEADS, HEAD_DIM, HEAD_DIM, W_HEADS, HEAD_DIM, N_HEADS, W_HEADS, W_HEADS, W_HEADS, d, d)
    offs = [0]
    for s in sizes:
        offs.append(offs[-1] + s)
    seg = lambda a, i: a[..., offs[i]:offs[i + 1]]
    order = (0, 3, 6, 7, 8, 9, 10, 1, 2, 4, 5)
    pad = SMALL_W - (3 * HEAD_DIM + N_HEADS)
    w_big = jnp.concatenate([seg(w_in_l, i) for i in order] + [jnp.zeros((d, pad), w_in_l.dtype)], axis=1)
    b_big = jnp.concatenate([seg(b_in_l, i) for i in order] + [jnp.zeros((pad,), b_in_l.dtype)], axis=0)
    return w_big.astype(BF16), b_big.reshape(1, -1).astype(F32)


def kernel(x, c, positions, w_ada, b_ada, g_mix, w_in, b_in, qn_a, kn_a, qn_b, kn_b, rel_bias,
           w_oa, w_ob, w_out, g_ffn, w_gu, w_down):
    b, seq, d = x.shape
    depth = w_ada.shape[0]
    t = b * seq
    top_k = min(TOPK_MAX, seq // 4)
    assert seq % KEY_CHUNK == 0 and seq % ROW_TILE == 0 and seq % BAND_Q == 0 and d % LANES_V7X == 0

    mod = _modulation(c, w_ada, b_ada).reshape(depth, b, 6, d)
    tabs = _rope_tables(positions)
    group = jnp.arange(W_HEADS) // HEAD_DIM
    gmat = jnp.where(group[:, None] == group[None, :], 1.0 / HEAD_DIM, 0.0).astype(BF16)
    lane = jnp.arange(LANES_V7X)
    tile_heads = lambda g: jnp.tile(g.astype(F32), N_HEADS)

    xf = x.reshape(t, d)
    for l in range(depth):
        w_big, b_big = _pack_in_projection(w_in[l], b_in[l], d)
        q_scale = ATTN_SCALE * LOG2_E
        gains = jnp.stack([tile_heads(qn_a[l]) * q_scale, tile_heads(qn_b[l]) * q_scale, tile_heads(kn_b[l])])
        sgains = jnp.stack([
            jnp.concatenate([kn_a[l].astype(F32), jnp.ones((LANES_V7X - HEAD_DIM,), F32)]),
            jnp.where((lane >= HEAD_DIM) & (lane < HEAD_DIM + N_HEADS), INDEX_SCALE, 1.0).astype(F32)])
        qa, qi, qb, kb, vb, ga, gb, sm = _in_projection(
            xf, mod[l], g_mix[l].reshape(1, d), w_big, b_big, gmat, tabs, gains, sgains, seq)

        ka = sm[:, 0:HEAD_DIM].astype(BF16).reshape(b, seq, HEAD_DIM)
        va_t = sm[:, HEAD_DIM:2 * HEAD_DIM].astype(BF16).reshape(b, seq // KEY_CHUNK, KEY_CHUNK, HEAD_DIM)
        va_t = va_t.transpose(0, 1, 3, 2)
        ki = sm[:, 2 * HEAD_DIM:3 * HEAD_DIM].astype(BF16).reshape(b, seq, HEAD_DIM)
        w_t = sm[:, 3 * HEAD_DIM:3 * HEAD_DIM + N_HEADS].reshape(b, seq // Q_BLOCK, Q_BLOCK, N_HEADS)
        w_t = w_t.transpose(0, 1, 3, 2)

        to3 = lambda a: a.reshape(b, seq, W_HEADS)
        attn_a = _sparse_attention(to3(qi), to3(qa), w_t, ki, ka, va_t, top_k)
        pm4 = lambda a: a.reshape(HEAD_PAIRS, b, seq, LANES_V7X)
        vb_t = vb.reshape(HEAD_PAIRS, b, seq // BAND_Q, BAND_Q, LANES_V7X).transpose(1, 0, 2, 4, 3)
        attn_b = _band_attention(pm4(qb), pm4(kb), vb_t, _band_bias_t(rel_bias[l]))

        xf = _out_ffn(xf, attn_a.reshape(t, W_HEADS), attn_b.reshape(HEAD_PAIRS, t, LANES_V7X), ga, gb, mod[l],
                      w_oa[l].astype(BF16), w_ob[l].astype(BF16), w_out[l].astype(BF16),
                      g_ffn[l].reshape(1, d), w_gu[l].astype(BF16), w_down[l].astype(BF16), seq)
    return xf.reshape(b, seq, d)
```

```python
import functools

import jax
import jax.numpy as jnp
from jax import lax
from jax.experimental import pallas as pl
from jax.experimental.pallas import tpu as pltpu

F32 = jnp.float32
BF16 = jnp.bfloat16
I32 = jnp.int32

CHUNK = 64
CHUNK_SHIFT = CHUNK.bit_length() - 1
HEAD_DIM = 64
N_HEADS = 8
TOPK_MAX = 256
N_PAST_CHUNKS = 8
REL_LO = -(CHUNK - 1)
REL_HI = 256
ROT_DIM = HEAD_DIM // 4
ROPE_THETA = 500000.0
EPS = 1e-6
W_HEADS = N_HEADS * HEAD_DIM
INDEX_SCALE = (HEAD_DIM ** -0.5) * (N_HEADS ** -0.5)
ATTN_SCALE = HEAD_DIM ** -0.5
LOG2_E = 1.4426950408889634
HEAD_PAIRS = N_HEADS // 2

LANES_V7X = 128
SUBLANES_V7X = 8
VMEM_BYTES_V7X = 64 * 1024 * 1024
VMEM_LIMIT = VMEM_BYTES_V7X * 3 // 4
VMEM_LIMIT_FFN = VMEM_BYTES_V7X * 7 // 8

Q_BLOCK = 128
KEY_CHUNK = 512
SMALL_W = 256
ROW_TILE = 256
INT_MIN = -(2 ** 31)
RANK_LOWEST_FINITE = INT_MIN + (1 << 23)
NEG_BIG = -1e30


def _dot(a, b):
    return jnp.dot(a, b, preferred_element_type=F32)


def _dot_nt(a, b):
    return lax.dot_general(a, b, (((1,), (1,)), ((), ())), preferred_element_type=F32)


def _mod_kernel(c_ref, w_ref, b_ref, o_ref):
    c = c_ref[...]
    c_act = (c * jax.nn.sigmoid(c)).astype(BF16)
    o_ref[0] = _dot(c_act, w_ref[0].astype(BF16)) + b_ref[0]


def _modulation(c, w_ada, b_ada):
    depth, d, n = w_ada.shape
    b = c.shape[0]
    tn = n // 6
    return pl.pallas_call(
        _mod_kernel,
        grid=(depth, n // tn),
        in_specs=[
            pl.BlockSpec((b, d), lambda l, j: (0, 0)),
            pl.BlockSpec((1, d, tn), lambda l, j: (l, 0, j)),
            pl.BlockSpec((1, 1, tn), lambda l, j: (l, 0, j)),
        ],
        out_specs=pl.BlockSpec((1, b, tn), lambda l, j: (l, 0, j)),
        out_shape=jax.ShapeDtypeStruct((depth, b, n), F32),
        name="adaln_modulation",
    )(c, w_ada, b_ada.reshape(depth, 1, n))


def _group_mean_sq(v, gmat):
    sq = v * v
    hi = sq.astype(BF16)
    lo = (sq - hi.astype(F32)).astype(BF16)
    return _dot(hi, gmat) + _dot(lo, gmat)


def _rope_cols(v, tab_c, tab_s1, tab_s2):
    cols = []
    for c0 in range(0, v.shape[1], LANES_V7X):
        vc = v[:, c0:c0 + LANES_V7X]
        fwd = pltpu.roll(vc, LANES_V7X - ROT_DIM // 2, axis=1)
        bwd = pltpu.roll(vc, ROT_DIM // 2, axis=1)
        cols.append(vc * tab_c + fwd * tab_s1 + bwd * tab_s2)
    return cols[0] if len(cols) == 1 else jnp.concatenate(cols, axis=1)


def _inproj_kernel(x_ref, mod_ref, g_ref, w_ref, b_ref, gmat_ref, tc_ref, ts1_ref, ts2_ref,
                   gain_ref, sgain_ref,
                   qa_ref, qi_ref, qb_ref, kb_ref, vb_ref, ga_ref, gb_ref, sm_ref):
    d = x_ref.shape[1]
    x = x_ref[...]
    y = x * lax.rsqrt(jnp.mean(x * x, axis=-1, keepdims=True) + EPS) * g_ref[...]
    h = y * (1.0 + mod_ref[0, 1:2, :]) + mod_ref[0, 0:1, :]
    z = _dot(h.astype(BF16), w_ref[...]) + b_ref[...]

    gmat = gmat_ref[...]
    tab_c, tab_s1, tab_s2 = tc_ref[...], ts1_ref[...], ts2_ref[...]
    w = W_HEADS

    def head_norm(v, gain):
        return v * lax.rsqrt(_group_mean_sq(v, gmat) + EPS) * gain

    qa = head_norm(z[:, 0:w], gain_ref[0:1, :])
    qa_ref[...] = _rope_cols(qa, tab_c, tab_s1, tab_s2).astype(BF16)
    qi_ref[...] = _rope_cols(z[:, w:2 * w], tab_c, tab_s1, tab_s2).astype(BF16)
    qb = head_norm(z[:, 2 * w:3 * w], gain_ref[1:2, :]).astype(BF16)
    kb = head_norm(z[:, 3 * w:4 * w], gain_ref[2:3, :]).astype(BF16)
    vb = z[:, 4 * w:5 * w].astype(BF16)
    for pr in range(HEAD_PAIRS):
        cols = slice(pr * LANES_V7X, (pr + 1) * LANES_V7X)
        qb_ref[pr] = qb[:, cols]
        kb_ref[pr] = kb[:, cols]
        vb_ref[pr] = vb[:, cols]
    ga_ref[...] = jax.nn.sigmoid(z[:, 5 * w:5 * w + d]).astype(BF16)
    gb_ref[...] = jax.nn.sigmoid(z[:, 5 * w + d:5 * w + 2 * d]).astype(BF16)

    sm = z[:, 5 * w + 2 * d:5 * w + 2 * d + SMALL_W]
    lane = lax.broadcasted_iota(I32, (1, LANES_V7X), 1)
    first_head = lane < HEAD_DIM
    c0 = sm[:, 0:LANES_V7X]
    c1 = sm[:, LANES_V7X:2 * LANES_V7X]
    ms0 = _group_mean_sq(c0, gmat[0:LANES_V7X, 0:LANES_V7X])
    c0 = jnp.where(first_head, c0 * lax.rsqrt(ms0 + EPS) * sgain_ref[0:1, :], c0)
    c1 = c1 * sgain_ref[1:2, :]
    t_c = jnp.where(first_head, tab_c, 1.0)
    t_s1 = jnp.where(first_head, tab_s1, 0.0)
    t_s2 = jnp.where(first_head, tab_s2, 0.0)
    sm_ref[:, 0:LANES_V7X] = _rope_cols(c0, t_c, t_s1, t_s2)
    sm_ref[:, LANES_V7X:2 * LANES_V7X] = _rope_cols(c1, t_c, t_s1, t_s2)


def _const_spec(shape):
    nd = len(shape)
    return pl.BlockSpec(shape, lambda i: (0,) * nd, pipeline_mode=pl.Buffered(1))


def _in_projection(xf, mod_l, g_mix, w_big, b_big, gmat, tabs, gains, sgains, seq):
    t, d = xf.shape
    tm = ROW_TILE
    tiles_per_batch = seq // tm
    n_big = w_big.shape[1]
    row = lambda width: pl.BlockSpec((tm, width), lambda i: (i, 0))
    pair_major = pl.BlockSpec((HEAD_PAIRS, tm, LANES_V7X), lambda i: (0, i, 0))
    pm_shape = jax.ShapeDtypeStruct((HEAD_PAIRS, t, LANES_V7X), BF16)
    return pl.pallas_call(
        _inproj_kernel,
        grid=(t // tm,),
        in_specs=[
            row(d),
            pl.BlockSpec((1, 6, d), lambda i: (i // tiles_per_batch, 0, 0)),
            _const_spec((1, d)),
            _const_spec((d, n_big)),
            _const_spec((1, n_big)),
            _const_spec((W_HEADS, W_HEADS)),
            row(LANES_V7X), row(LANES_V7X), row(LANES_V7X),
            _const_spec((3, W_HEADS)),
            _const_spec((2, LANES_V7X)),
        ],
        out_specs=[row(W_HEADS), row(W_HEADS), pair_major, pair_major, pair_major, row(d), row(d), row(SMALL_W)],
        out_shape=[jax.ShapeDtypeStruct((t, W_HEADS), BF16)] * 2 + [pm_shape] * 3
        + [jax.ShapeDtypeStruct((t, d), BF16)] * 2 + [jax.ShapeDtypeStruct((t, SMALL_W), F32)],
        compiler_params=pltpu.CompilerParams(
            dimension_semantics=("arbitrary",), vmem_limit_bytes=VMEM_LIMIT),
        name="in_projection",
    )(xf, mod_l, g_mix, w_big, b_big, gmat, *tabs, gains, sgains)


def _sparse_kernel(qi_ref, qa_ref, wt_ref, ki_ref, ka_ref, vt_ref, o_ref,
                   qi_s, qa_s, score_s, s_scr, acc_s, *, top_k):
    seq = qi_ref.shape[1]
    n_qb = seq // Q_BLOCK
    kc_rows = KEY_CHUNK
    sub_groups = kc_rows // SUBLANES_V7X

    def q_block(qb, carry):
        r0 = pl.multiple_of(qb * Q_BLOCK, Q_BLOCK)
        qi = qi_ref[0, pl.ds(r0, Q_BLOCK), :]
        qa = qa_ref[0, pl.ds(r0, Q_BLOCK), :]
        for h in range(N_HEADS):
            qi_s[h * Q_BLOCK:(h + 1) * Q_BLOCK, :] = qi[:, h * HEAD_DIM:(h + 1) * HEAD_DIM]
            qa_s[h * Q_BLOCK:(h + 1) * Q_BLOCK, :] = qa[:, h * HEAD_DIM:(h + 1) * HEAD_DIM]
        w_idx = wt_ref[0, qb]
        n_chunks = (r0 + Q_BLOCK + kc_rows - 1) // kc_rows
        lane = lax.broadcasted_iota(I32, (1, Q_BLOCK), 1)
        limit = (((r0 + lane) >> CHUNK_SHIFT) + 1) * CHUNK

        row_id = lax.broadcasted_iota(I32, (kc_rows, Q_BLOCK), 0)

        def score_chunk(j, c):
            k0 = pl.multiple_of(j * kc_rows, kc_rows)
            logits = _dot_nt(ki_ref[0, pl.ds(k0, kc_rows), :], qi_s[...])
            score = jnp.zeros((kc_rows, Q_BLOCK), F32)
            for h in range(N_HEADS):
                score = score + jnp.maximum(logits[:, h * Q_BLOCK:(h + 1) * Q_BLOCK], 0.0) * w_idx[h:h + 1, :]
            score_s[pl.ds(k0, kc_rows), :] = jnp.where(row_id < limit - k0, score, -jnp.inf)
            return c

        lax.fori_loop(0, n_chunks, score_chunk, 0)

        def count(pred_fn):
            def body(j, acc):
                k0 = pl.multiple_of(j * kc_rows, kc_rows)
                hit = jnp.where(pred_fn(score_s[pl.ds(k0, kc_rows), :]), 1, 0).astype(I32)
                return acc + hit.reshape(sub_groups, SUBLANES_V7X, Q_BLOCK).sum(axis=0)
            acc = lax.fori_loop(0, n_chunks, body, jnp.zeros((SUBLANES_V7X, Q_BLOCK), I32))
            return acc.sum(axis=0, keepdims=True)

        def float_of(key):
            return jnp.broadcast_to(
                lax.bitcast_convert_type(key ^ ((key >> 31) & 0x7FFFFFFF), F32), (kc_rows, Q_BLOCK))

        def search_bit(i, state):
            thr, cnt_thr = state
            cand = thr + jnp.left_shift(jnp.int32(1), 31 - i)
            cand_f = float_of(cand)
            cnt = count(lambda sc: sc >= cand_f)
            take = cnt >= top_k
            return jnp.where(take, cand, thr), jnp.where(take, cnt, cnt_thr)

        thr0 = jnp.full((1, Q_BLOCK), INT_MIN, I32)
        cnt0 = jnp.broadcast_to(n_chunks * kc_rows, (1, Q_BLOCK)).astype(I32)
        need_search = r0 + Q_BLOCK > top_k
        thr, cnt_thr = lax.fori_loop(0, jnp.where(need_search, 32, 0), search_bit, (thr0, cnt0))
        cnt_thr = jnp.where(need_search, cnt_thr, top_k)
        thr = jnp.maximum(thr, RANK_LOWEST_FINITE)
        thr_f = float_of(thr)

        @pl.when(jnp.max(cnt_thr) > top_k)
        def _():
            next_f = float_of(thr + 1)
            need = (top_k - count(lambda sc: sc >= next_f)).astype(F32)
            r_i = lax.broadcasted_iota(I32, (kc_rows, kc_rows), 0)
            c_i = lax.broadcasted_iota(I32, (kc_rows, kc_rows), 1)
            below = jnp.where(c_i < r_i, 1.0, 0.0).astype(BF16)

            def drop_chunk(j, seen):
                k0 = pl.multiple_of(j * kc_rows, kc_rows)
                sc = score_s[pl.ds(k0, kc_rows), :]
                eq = (sc >= thr_f) & (sc < next_f)
                eq_f = jnp.where(eq, 1.0, 0.0)
                rank = _dot(below, eq_f.astype(BF16)) + seen
                score_s[pl.ds(k0, kc_rows), :] = jnp.where(eq & (rank >= need), -jnp.inf, sc)
                return seen + jnp.sum(eq_f, axis=0, keepdims=True)

            lax.fori_loop(0, n_chunks, drop_chunk, jnp.zeros((1, Q_BLOCK), F32))

        wide = N_HEADS * Q_BLOCK

        def masked_logits(j, m_acc):
            k0 = pl.multiple_of(j * kc_rows, kc_rows)
            s_all = _dot_nt(ka_ref[0, pl.ds(k0, kc_rows), :], qa_s[...])
            sel = score_s[pl.ds(k0, kc_rows), :] >= thr_f
            s = jnp.concatenate(
                [jnp.where(sel, s_all[:, h * Q_BLOCK:(h + 1) * Q_BLOCK], -jnp.inf) for h in range(N_HEADS)],
                axis=1)
            s_scr[pl.ds(k0, kc_rows), :] = s
            return jnp.maximum(m_acc, s.reshape(sub_groups, SUBLANES_V7X, wide).max(axis=0))

        m_acc = lax.fori_loop(0, n_chunks, masked_logits,
                              jnp.full((SUBLANES_V7X, wide), -jnp.inf, F32))
        m_fin = jnp.max(m_acc, axis=0, keepdims=True)
        acc_s[...] = jnp.zeros_like(acc_s)

        def weigh_values(j, l_acc):
            k0 = pl.multiple_of(j * kc_rows, kc_rows)
            p = jnp.exp2(s_scr[pl.ds(k0, kc_rows), :] - m_fin)
            acc_s[...] += _dot(vt_ref[0, j], p.astype(BF16))
            return l_acc + p.reshape(sub_groups, SUBLANES_V7X, wide).sum(axis=0)

        l_acc = lax.fori_loop(0, n_chunks, weigh_values, jnp.zeros((SUBLANES_V7X, wide), F32))
        o_t = acc_s[...] / jnp.sum(l_acc, axis=0, keepdims=True)
        outs = [o_t[:, h * Q_BLOCK:(h + 1) * Q_BLOCK].T for h in range(N_HEADS)]
        o_ref[0, pl.ds(r0, Q_BLOCK), :] = jnp.concatenate(outs, axis=1).astype(BF16)
        return carry

    lax.fori_loop(0, n_qb, q_block, 0)


def _sparse_attention(qi, qa, w_t, ki, ka, v_t, top_k):
    b, seq, w = qi.shape
    batch3 = lambda s1, s2: pl.BlockSpec((1, s1, s2), lambda i: (i, 0, 0))
    batch4 = lambda s1, s2, s3: pl.BlockSpec((1, s1, s2, s3), lambda i: (i, 0, 0, 0))
    return pl.pallas_call(
        functools.partial(_sparse_kernel, top_k=top_k),
        grid=(b,),
        in_specs=[
            batch3(seq, w), batch3(seq, w),
            batch4(seq // Q_BLOCK, N_HEADS, Q_BLOCK),
            batch3(seq, HEAD_DIM), batch3(seq, HEAD_DIM),
            batch4(seq // KEY_CHUNK, HEAD_DIM, KEY_CHUNK),
        ],
        out_specs=batch3(seq, w),
        out_shape=jax.ShapeDtypeStruct((b, seq, w), BF16),
        scratch_shapes=[
            pltpu.VMEM((N_HEADS * Q_BLOCK, HEAD_DIM), BF16),
            pltpu.VMEM((N_HEADS * Q_BLOCK, HEAD_DIM), BF16),
            pltpu.VMEM((seq, Q_BLOCK), F32),
            pltpu.VMEM((seq, N_HEADS * Q_BLOCK), F32),
            pltpu.VMEM((HEAD_DIM, N_HEADS * Q_BLOCK), F32),
        ],
        compiler_params=pltpu.CompilerParams(
            dimension_semantics=("arbitrary",), vmem_limit_bytes=VMEM_LIMIT),
        name="sparse_attention",
    )(qi, qa, w_t, ki, ka, v_t)


BAND_Q = 256
BAND_KEYS = N_PAST_CHUNKS * CHUNK + BAND_Q
BAND_PIECES = BAND_KEYS // BAND_Q


def _band_kernel(q_ref, k_ref, vt_ref, bias_ref, o_ref, s_a, s_b):
    seq = q_ref.shape[2]
    n_items = (seq // BAND_Q) * HEAD_PAIRS
    groups = BAND_Q // SUBLANES_V7X
    lane = lax.broadcasted_iota(I32, (1, LANES_V7X), 1)
    head_lanes = (lane < HEAD_DIM, lane >= HEAD_DIM)

    def pieces_of(qb):
        out = []
        for w in range(BAND_PIECES):
            kb_idx = qb - (BAND_PIECES - 1) + w
            kb_c = jnp.maximum(kb_idx, 0)
            out.append((kb_c, pl.multiple_of(kb_c * BAND_Q, BAND_Q),
                        jnp.where(kb_idx >= 0, 0.0, NEG_BIG)))
        return out

    def logits(item, s_scr):
        qb, pr = item // HEAD_PAIRS, item % HEAD_PAIRS
        r0 = pl.multiple_of(qb * BAND_Q, BAND_Q)
        q2 = q_ref[pr, 0, pl.ds(r0, BAND_Q), :]
        for e in range(2):
            q_h = jnp.where(head_lanes[e], q2, jnp.zeros_like(q2))
            for w, (_, row, _) in enumerate(pieces_of(qb)):
                rows = slice(w * BAND_Q, (w + 1) * BAND_Q)
                k2 = k_ref[pr, 0, pl.ds(row, BAND_Q), :]
                s_scr[e, rows, :] = _dot_nt(k2, q_h) + bias_ref[2 * pr + e, rows, :]

    def softmax_values(item, s_scr):
        qb, pr = item // HEAD_PAIRS, item % HEAD_PAIRS
        r0 = pl.multiple_of(qb * BAND_Q, BAND_Q)
        pieces = pieces_of(qb)
        o_pair = []
        for e in range(2):
            m_acc = jnp.full((SUBLANES_V7X, BAND_Q), -jnp.inf, F32)
            for w, (_, _, pen) in enumerate(pieces):
                rows = slice(w * BAND_Q, (w + 1) * BAND_Q)
                m_acc = jnp.maximum(
                    m_acc, s_scr[e, rows, :].reshape(groups, SUBLANES_V7X, BAND_Q).max(axis=0) + pen)
            m_fin = jnp.max(m_acc, axis=0, keepdims=True)
            l_acc = jnp.zeros((SUBLANES_V7X, BAND_Q), F32)
            o_t = jnp.zeros((HEAD_DIM, BAND_Q), F32)
            for w, (kb_c, _, pen) in enumerate(pieces):
                rows = slice(w * BAND_Q, (w + 1) * BAND_Q)
                p = jnp.exp2(s_scr[e, rows, :] - (m_fin - pen))
                l_acc = l_acc + p.reshape(groups, SUBLANES_V7X, BAND_Q).sum(axis=0)
                v_h = vt_ref[0, pr, kb_c][e * HEAD_DIM:(e + 1) * HEAD_DIM, :]
                o_t = o_t + _dot(v_h, p.astype(BF16))
            o_pair.append(o_t / jnp.sum(l_acc, axis=0, keepdims=True))
        o_ref[pr, 0, pl.ds(r0, BAND_Q), :] = jnp.concatenate(o_pair, axis=0).T.astype(BF16)

    logits(0, s_a)

    def step(u, c):
        logits(2 * u + 1, s_b)
        softmax_values(2 * u, s_a)
        logits(2 * u + 2, s_a)
        softmax_values(2 * u + 1, s_b)
        return c

    lax.fori_loop(0, n_items // 2 - 1, step, 0)
    logits(n_items - 1, s_b)
    softmax_values(n_items - 2, s_a)
    softmax_values(n_items - 1, s_b)


def _band_attention(qb_pm, kb_pm, vb_t, bias_t):
    pairs, b, seq, lanes = qb_pm.shape
    batch4 = pl.BlockSpec((pairs, 1, seq, lanes), lambda i: (0, i, 0, 0))
    return pl.pallas_call(
        _band_kernel,
        grid=(b,),
        in_specs=[batch4, batch4,
                  pl.BlockSpec((1,) + vb_t.shape[1:], lambda i: (i, 0, 0, 0, 0)),
                  _const_spec(bias_t.shape)],
        out_specs=batch4,
        out_shape=jax.ShapeDtypeStruct(qb_pm.shape, BF16),
        scratch_shapes=[pltpu.VMEM((2, BAND_KEYS, BAND_Q), F32)] * 2,
        compiler_params=pltpu.CompilerParams(
            dimension_semantics=("arbitrary",), vmem_limit_bytes=VMEM_LIMIT),
        name="band_attention",
    )(qb_pm, kb_pm, vb_t, bias_t)


def _bias_table_kernel(vec_ref, o_ref):
    period = vec_ref.shape[2]
    base = pltpu.roll(jnp.broadcast_to(vec_ref[0], (SUBLANES_V7X, period)), 0, 1, stride=1, stride_axis=0)
    q_chunk = lax.broadcasted_iota(I32, (SUBLANES_V7X, BAND_Q), 1) >> CHUNK_SHIFT

    def rows(c, carry):
        j0 = pl.multiple_of(c * SUBLANES_V7X, SUBLANES_V7X)
        blk = pltpu.roll(base, j0, 1)[:, :BAND_Q]
        k_chunk = j0 >> CHUNK_SHIFT
        in_band = (q_chunk <= k_chunk) & (q_chunk + N_PAST_CHUNKS >= k_chunk)
        o_ref[0, pl.ds(j0, SUBLANES_V7X), :] = jnp.where(in_band, blk * LOG2_E, NEG_BIG)
        return carry

    lax.fori_loop(0, BAND_KEYS // SUBLANES_V7X, rows, 0)


def _band_bias_t(rel_bias_l):
    n_rel = REL_HI - REL_LO + 1
    period = BAND_KEYS + BAND_Q
    pad = N_PAST_CHUNKS * CHUNK
    lo_pos = period - pad + REL_LO
    rb = rel_bias_l.astype(F32)
    hi_val, lo_val = rb[:, n_rel - 1:n_rel], rb[:, 0:1]
    vec = jnp.concatenate([
        jnp.broadcast_to(hi_val, (N_HEADS, BAND_Q + 1)),
        jnp.broadcast_to(lo_val, (N_HEADS, lo_pos - (BAND_Q + 1))),
        rb,
        jnp.broadcast_to(hi_val, (N_HEADS, period - lo_pos - n_rel)),
    ], axis=1)
    assert vec.shape[1] == period and period % LANES_V7X == 0
    return pl.pallas_call(
        _bias_table_kernel,
        grid=(N_HEADS,),
        in_specs=[pl.BlockSpec((1, 1, period), lambda h: (h, 0, 0))],
        out_specs=pl.BlockSpec((1, BAND_KEYS, BAND_Q), lambda h: (h, 0, 0)),
        out_shape=jax.ShapeDtypeStruct((N_HEADS, BAND_KEYS, BAND_Q), F32),
        name="band_bias_table",
    )(vec.reshape(N_HEADS, 1, period))


def _out_ffn_kernel(x_ref, a_ref, b_ref, ga_ref, gb_ref, mod_ref, woa_ref, wob_ref, wout_ref,
                    g_ref, wgu_ref, wdown_ref, o_ref):
    d_ff = wdown_ref.shape[0]
    y_a = _dot(a_ref[...], woa_ref[...])
    attn_b = jnp.concatenate([b_ref[pr] for pr in range(HEAD_PAIRS)], axis=1)
    y_b = _dot(attn_b, wob_ref[...])
    merged = ga_ref[...].astype(F32) * y_a + gb_ref[...].astype(F32) * y_b
    x1 = x_ref[...] + mod_ref[0, 2:3, :] * _dot(merged.astype(BF16), wout_ref[...])

    y = x1 * lax.rsqrt(jnp.mean(x1 * x1, axis=-1, keepdims=True) + EPS) * g_ref[...]
    h2 = y * (1.0 + mod_ref[0, 4:5, :]) + mod_ref[0, 3:4, :]
    gu = _dot(h2.astype(BF16), wgu_ref[...])
    gate, up = gu[:, 0:d_ff], gu[:, d_ff:2 * d_ff]
    act = gate * jax.nn.sigmoid(gate) * up
    o_ref[...] = x1 + mod_ref[0, 5:6, :] * _dot(act.astype(BF16), wdown_ref[...])


def _out_ffn(xf, attn_a, attn_b, ga, gb, mod_l, w_oa, w_ob, w_out, g_ffn, w_gu, w_down, seq):
    t, d = xf.shape
    tm = ROW_TILE
    tiles_per_batch = seq // tm
    row = lambda width: pl.BlockSpec((tm, width), lambda i: (i, 0))
    return pl.pallas_call(
        _out_ffn_kernel,
        grid=(t // tm,),
        in_specs=[
            row(d), row(W_HEADS), pl.BlockSpec((HEAD_PAIRS, tm, LANES_V7X), lambda i: (0, i, 0)), row(d), row(d),
            pl.BlockSpec((1, 6, d), lambda i: (i // tiles_per_batch, 0, 0)),
            _const_spec(w_oa.shape), _const_spec(w_ob.shape), _const_spec(w_out.shape),
            _const_spec((1, d)), _const_spec(w_gu.shape), _const_spec(w_down.shape),
        ],
        out_specs=row(d),
        out_shape=jax.ShapeDtypeStruct((t, d), F32),
        compiler_params=pltpu.CompilerParams(
            dimension_semantics=("arbitrary",), vmem_limit_bytes=VMEM_LIMIT_FFN),
        name="out_proj_ffn",
    )(xf, attn_a, attn_b, ga, gb, mod_l, w_oa, w_ob, w_out, g_ffn, w_gu, w_down)


def _rope_tables(positions):
    half = ROT_DIM // 2
    inv_freq = ROPE_THETA ** (-jnp.arange(0, ROT_DIM, 2, dtype=F32) / ROT_DIM)
    ang = positions.astype(F32).reshape(-1, 1) * inv_freq
    cos, sin = jnp.cos(ang), jnp.sin(ang)
    t = cos.shape[0]
    ones = jnp.ones((t, HEAD_DIM - ROT_DIM), F32)
    zeros = jnp.zeros((t, HEAD_DIM - half), F32)
    tab_c = jnp.concatenate([cos, cos, ones], axis=1)
    tab_s1 = jnp.concatenate([-sin, zeros], axis=1)
    tab_s2 = jnp.concatenate([jnp.zeros((t, half), F32), sin, zeros[:, half:]], axis=1)
    rep = LANES_V7X // HEAD_DIM
    return tuple(jnp.tile(tb, (1, rep)) for tb in (tab_c, tab_s1, tab_s2))


def _pack_in_projection(w_in_l, b_in_l, d):
    sizes = (W_HEADS, HEAD_DIM, HEAD_DIM, W_HEADS, HEAD_DIM, N_HEADS, W_HEADS, W_HEADS, W_HEADS, d, d)
    offs = [0]
    for s in sizes:
        offs.append(offs[-1] + s)
    seg = lambda a, i: a[..., offs[i]:offs[i + 1]]
    order = (0, 3, 6, 7, 8, 9, 10, 1, 2, 4, 5)
    pad = SMALL_W - (3 * HEAD_DIM + N_HEADS)
    w_big = jnp.concatenate([seg(w_in_l, i) for i in order] + [jnp.zeros((d, pad), w_in_l.dtype)], axis=1)
    b_big = jnp.concatenate([seg(b_in_l, i) for i in order] + [jnp.zeros((pad,), b_in_l.dtype)], axis=0)
    return w_big.astype(BF16), b_big.reshape(1, -1).astype(F32)


def kernel(x, c, positions, w_ada, b_ada, g_mix, w_in, b_in, qn_a, kn_a, qn_b, kn_b, rel_bias,
           w_oa, w_ob, w_out, g_ffn, w_gu, w_down):
    b, seq, d = x.shape
    depth = w_ada.shape[0]
    t = b * seq
    top_k = min(TOPK_MAX, seq // 4)
    assert seq % KEY_CHUNK == 0 and seq % ROW_TILE == 0 and seq % BAND_Q == 0 and d % LANES_V7X == 0

    mod = _modulation(c, w_ada, b_ada).reshape(depth, b, 6, d)
    tabs = _rope_tables(positions)
    group = jnp.arange(W_HEADS) // HEAD_DIM
    gmat = jnp.where(group[:, None] == group[None, :], 1.0 / HEAD_DIM, 0.0).astype(BF16)
    lane = jnp.arange(LANES_V7X)
    tile_heads = lambda g: jnp.tile(g.astype(F32), N_HEADS)

    xf = x.reshape(t, d)
    for l in range(depth):
        w_big, b_big = _pack_in_projection(w_in[l], b_in[l], d)
        q_scale = ATTN_SCALE * LOG2_E
        gains = jnp.stack([tile_heads(qn_a[l]) * q_scale, tile_heads(qn_b[l]) * q_scale, tile_heads(kn_b[l])])
        sgains = jnp.stack([
            jnp.concatenate([kn_a[l].astype(F32), jnp.ones((LANES_V7X - HEAD_DIM,), F32)]),
            jnp.where((lane >= HEAD_DIM) & (lane < HEAD_DIM + N_HEADS), INDEX_SCALE, 1.0).astype(F32)])
        qa, qi, qb, kb, vb, ga, gb, sm = _in_projection(
            xf, mod[l], g_mix[l].reshape(1, d), w_big, b_big, gmat, tabs, gains, sgains, seq)

        ka = sm[:, 0:HEAD_DIM].astype(BF16).reshape(b, seq, HEAD_DIM)
        va_t = sm[:, HEAD_DIM:2 * HEAD_DIM].astype(BF16).reshape(b, seq // KEY_CHUNK, KEY_CHUNK, HEAD_DIM)
        va_t = va_t.transpose(0, 1, 3, 2)
        ki = sm[:, 2 * HEAD_DIM:3 * HEAD_DIM].astype(BF16).reshape(b, seq, HEAD_DIM)
        w_t = sm[:, 3 * HEAD_DIM:3 * HEAD_DIM + N_HEADS].reshape(b, seq // Q_BLOCK, Q_BLOCK, N_HEADS)
        w_t = w_t.transpose(0, 1, 3, 2)

        to3 = lambda a: a.reshape(b, seq, W_HEADS)
        attn_a = _sparse_attention(to3(qi), to3(qa), w_t, ki, ka, va_t, top_k)
        pm4 = lambda a: a.reshape(HEAD_PAIRS, b, seq, LANES_V7X)
        vb_t = vb.reshape(HEAD_PAIRS, b, seq // BAND_Q, BAND_Q, LANES_V7X).transpose(1, 0, 2, 4, 3)
        attn_b = _band_attention(pm4(qb), pm4(kb), vb_t, _band_bias_t(rel_bias[l]))

        xf = _out_ffn(xf, attn_a.reshape(t, W_HEADS), attn_b.reshape(HEAD_PAIRS, t, LANES_V7X), ga, gb, mod[l],
                      w_oa[l].astype(BF16), w_ob[l].astype(BF16), w_out[l].astype(BF16),
                      g_ffn[l].reshape(1, d), w_gu[l].astype(BF16), w_down[l].astype(BF16), seq)
    return xf.reshape(b, seq, d)
```

```python
import functools

import jax
import jax.numpy as jnp
from jax import lax
from jax.experimental import pallas as pl
from jax.experimental.pallas import tpu as pltpu

F32 = jnp.float32
BF16 = jnp.bfloat16
I32 = jnp.int32

CHUNK = 64
CHUNK_SHIFT = CHUNK.bit_length() - 1
HEAD_DIM = 64
N_HEADS = 8
TOPK_MAX = 256
N_PAST_CHUNKS = 8
REL_LO = -(CHUNK - 1)
REL_HI = 256
ROT_DIM = HEAD_DIM // 4
ROPE_THETA = 500000.0
EPS = 1e-6
W_HEADS = N_HEADS * HEAD_DIM
INDEX_SCALE = (HEAD_DIM ** -0.5) * (N_HEADS ** -0.5)
ATTN_SCALE = HEAD_DIM ** -0.5
LOG2_E = 1.4426950408889634
HEAD_PAIRS = N_HEADS // 2

LANES_V7X = 128
SUBLANES_V7X = 8
VMEM_BYTES_V7X = 64 * 1024 * 1024
VMEM_LIMIT = VMEM_BYTES_V7X * 3 // 4
VMEM_LIMIT_FFN = VMEM_BYTES_V7X * 7 // 8

Q_BLOCK = 128
KEY_CHUNK = 512
SMALL_W = 256
ROW_TILE = 256
INT_MIN = -(2 ** 31)
RANK_LOWEST_FINITE = INT_MIN + (1 << 23)
NEG_BIG = -1e30


def _dot(a, b):
    return jnp.dot(a, b, preferred_element_type=F32)


def _dot_nt(a, b):
    return lax.dot_general(a, b, (((1,), (1,)), ((), ())), preferred_element_type=F32)


def _mod_kernel(c_ref, w_ref, b_ref, o_ref):
    c = c_ref[...]
    c_act = (c * jax.nn.sigmoid(c)).astype(BF16)
    o_ref[0] = _dot(c_act, w_ref[0].astype(BF16)) + b_ref[0]


def _modulation(c, w_ada, b_ada):
    depth, d, n = w_ada.shape
    b = c.shape[0]
    tn = n // 6
    return pl.pallas_call(
        _mod_kernel,
        grid=(depth, n // tn),
        in_specs=[
            pl.BlockSpec((b, d), lambda l, j: (0, 0)),
            pl.BlockSpec((1, d, tn), lambda l, j: (l, 0, j)),
            pl.BlockSpec((1, 1, tn), lambda l, j: (l, 0, j)),
        ],
        out_specs=pl.BlockSpec((1, b, tn), lambda l, j: (l, 0, j)),
        out_shape=jax.ShapeDtypeStruct((depth, b, n), F32),
        name="adaln_modulation",
    )(c, w_ada, b_ada.reshape(depth, 1, n))


def _group_mean_sq(v, gmat):
    return _dot((v * v).astype(BF16), gmat)


def _rope_cols(v, tab_c, tab_s1, tab_s2):
    cols = []
    for c0 in range(0, v.shape[1], LANES_V7X):
        vc = v[:, c0:c0 + LANES_V7X]
        fwd = pltpu.roll(vc, LANES_V7X - ROT_DIM // 2, axis=1)
        bwd = pltpu.roll(vc, ROT_DIM // 2, axis=1)
        cols.append(vc * tab_c + fwd * tab_s1 + bwd * tab_s2)
    return cols[0] if len(cols) == 1 else jnp.concatenate(cols, axis=1)


def _inproj_kernel(x_ref, mod_ref, g_ref, w_ref, b_ref, gmat_ref, tc_ref, ts1_ref, ts2_ref,
                   gain_ref, sgain_ref,
                   qa_ref, qi_ref, qb_ref, kb_ref, vb_ref, ga_ref, gb_ref, sm_ref):
    d = x_ref.shape[1]
    x = x_ref[...]
    y = x * lax.rsqrt(jnp.mean(x * x, axis=-1, keepdims=True) + EPS) * g_ref[...]
    h = y * (1.0 + mod_ref[0, 1:2, :]) + mod_ref[0, 0:1, :]
    z = _dot(h.astype(BF16), w_ref[...]) + b_ref[...]

    gmat = gmat_ref[...]
    tab_c, tab_s1, tab_s2 = tc_ref[...], ts1_ref[...], ts2_ref[...]
    w = W_HEADS

    def head_norm(v, gain):
        return v * lax.rsqrt(_group_mean_sq(v, gmat) + EPS) * gain

    qa = head_norm(z[:, 0:w], gain_ref[0:1, :])
    qa_ref[...] = _rope_cols(qa, tab_c, tab_s1, tab_s2).astype(BF16)
    qi_ref[...] = _rope_cols(z[:, w:2 * w], tab_c, tab_s1, tab_s2).astype(BF16)
    qb = head_norm(z[:, 2 * w:3 * w], gain_ref[1:2, :]).astype(BF16)
    kb = head_norm(z[:, 3 * w:4 * w], gain_ref[2:3, :]).astype(BF16)
    vb = z[:, 4 * w:5 * w].astype(BF16)
    for pr in range(HEAD_PAIRS):
        cols = slice(pr * LANES_V7X, (pr + 1) * LANES_V7X)
        qb_ref[pr] = qb[:, cols]
        kb_ref[pr] = kb[:, cols]
        vb_ref[pr] = vb[:, cols]
    ga_ref[...] = jax.nn.sigmoid(z[:, 5 * w:5 * w + d]).astype(BF16)
    gb_ref[...] = jax.nn.sigmoid(z[:, 5 * w + d:5 * w + 2 * d]).astype(BF16)

    sm = z[:, 5 * w + 2 * d:5 * w + 2 * d + SMALL_W]
    lane = lax.broadcasted_iota(I32, (1, LANES_V7X), 1)
    first_head = lane < HEAD_DIM
    c0 = sm[:, 0:LANES_V7X]
    c1 = sm[:, LANES_V7X:2 * LANES_V7X]
    ms0 = _group_mean_sq(c0, gmat[0:LANES_V7X, 0:LANES_V7X])
    c0 = jnp.where(first_head, c0 * lax.rsqrt(ms0 + EPS) * sgain_ref[0:1, :], c0)
    c1 = c1 * sgain_ref[1:2, :]
    t_c = jnp.where(first_head, tab_c, 1.0)
    t_s1 = jnp.where(first_head, tab_s1, 0.0)
    t_s2 = jnp.where(first_head, tab_s2, 0.0)
    sm_ref[:, 0:LANES_V7X] = _rope_cols(c0, t_c, t_s1, t_s2)
    sm_ref[:, LANES_V7X:2 * LANES_V7X] = _rope_cols(c1, t_c, t_s1, t_s2)


def _const_spec(shape):
    nd = len(shape)
    return pl.BlockSpec(shape, lambda i: (0,) * nd, pipeline_mode=pl.Buffered(1))


def _in_projection(xf, mod_l, g_mix, w_big, b_big, gmat, tabs, gains, sgains, seq):
    t, d = xf.shape
    tm = ROW_TILE
    tiles_per_batch = seq // tm
    n_big = w_big.shape[1]
    row = lambda width: pl.BlockSpec((tm, width), lambda i: (i, 0))
    pair_major = pl.BlockSpec((HEAD_PAIRS, tm, LANES_V7X), lambda i: (0, i, 0))
    pm_shape = jax.ShapeDtypeStruct((HEAD_PAIRS, t, LANES_V7X), BF16)
    return pl.pallas_call(
        _inproj_kernel,
        grid=(t // tm,),
        in_specs=[
            row(d),
            pl.BlockSpec((1, 6, d), lambda i: (i // tiles_per_batch, 0, 0)),
            _const_spec((1, d)),
            _const_spec((d, n_big)),
            _const_spec((1, n_big)),
            _const_spec((W_HEADS, W_HEADS)),
            row(LANES_V7X), row(LANES_V7X), row(LANES_V7X),
            _const_spec((3, W_HEADS)),
            _const_spec((2, LANES_V7X)),
        ],
        out_specs=[row(W_HEADS), row(W_HEADS), pair_major, pair_major, pair_major, row(d), row(d), row(SMALL_W)],
        out_shape=[jax.ShapeDtypeStruct((t, W_HEADS), BF16)] * 2 + [pm_shape] * 3
        + [jax.ShapeDtypeStruct((t, d), BF16)] * 2 + [jax.ShapeDtypeStruct((t, SMALL_W), F32)],
        compiler_params=pltpu.CompilerParams(
            dimension_semantics=("arbitrary",), vmem_limit_bytes=VMEM_LIMIT),
        name="in_projection",
    )(xf, mod_l, g_mix, w_big, b_big, gmat, *tabs, gains, sgains)


def _sparse_kernel(qi_ref, qa_ref, wt_ref, ki_ref, ka_ref, vt_ref, o_ref,
                   qi_s, qa_s, score_s, s_scr, acc_s, tri_s, *, top_k):
    seq = qi_ref.shape[1]
    n_qb = seq // Q_BLOCK
    kc_rows = KEY_CHUNK
    sub_groups = kc_rows // SUBLANES_V7X
    wide = N_HEADS * Q_BLOCK
    r_i = lax.broadcasted_iota(I32, (kc_rows, kc_rows), 0)
    c_i = lax.broadcasted_iota(I32, (kc_rows, kc_rows), 1)
    tri_s[...] = jnp.where(c_i < r_i, 1.0, 0.0).astype(BF16)

    def float_of(key):
        return jnp.broadcast_to(
            lax.bitcast_convert_type(key ^ ((key >> 31) & 0x7FFFFFFF), F32), (kc_rows, Q_BLOCK))

    def attend(qb, n_chunks):
        r0 = pl.multiple_of(qb * Q_BLOCK, Q_BLOCK)
        chunks = [j * kc_rows for j in range(n_chunks)]
        w_idx = wt_ref[0, qb]
        lane = lax.broadcasted_iota(I32, (1, Q_BLOCK), 1)
        limit = (((r0 + lane) >> CHUNK_SHIFT) + 1) * CHUNK
        row_id = lax.broadcasted_iota(I32, (kc_rows, Q_BLOCK), 0)

        for k0 in chunks:
            logits = _dot_nt(ki_ref[0, k0:k0 + kc_rows, :], qi_s[...])
            score = jnp.zeros((kc_rows, Q_BLOCK), F32)
            for h in range(N_HEADS):
                score = score + jnp.maximum(logits[:, h * Q_BLOCK:(h + 1) * Q_BLOCK], 0.0) * w_idx[h:h + 1, :]
            if k0 + kc_rows > (n_chunks - 1) * kc_rows:
                score = jnp.where(row_id < limit - k0, score, -jnp.inf)
            score_s[k0:k0 + kc_rows, :] = score

        def count(pred_fn):
            acc = jnp.zeros((SUBLANES_V7X, Q_BLOCK), I32)
            for k0 in chunks:
                hit = jnp.where(pred_fn(score_s[k0:k0 + kc_rows, :]), 1, 0).astype(I32)
                acc = acc + hit.reshape(sub_groups, SUBLANES_V7X, Q_BLOCK).sum(axis=0)
            return acc.sum(axis=0, keepdims=True)

        def search_bit(i, state):
            thr, cnt_thr = state
            cand = thr + jnp.left_shift(jnp.int32(1), 31 - i)
            cand_f = float_of(cand)
            cnt = count(lambda sc: sc >= cand_f)
            take = cnt >= top_k
            return jnp.where(take, cand, thr), jnp.where(take, cnt, cnt_thr)

        thr0 = jnp.full((1, Q_BLOCK), INT_MIN, I32)
        cnt0 = jnp.full((1, Q_BLOCK), n_chunks * kc_rows, I32)
        need_search = r0 + Q_BLOCK > top_k
        thr, cnt_thr = lax.fori_loop(0, jnp.where(need_search, 32, 0), search_bit, (thr0, cnt0))
        cnt_thr = jnp.where(need_search, cnt_thr, top_k)
        thr = jnp.maximum(thr, RANK_LOWEST_FINITE)
        thr_f = float_of(thr)

        @pl.when(jnp.max(cnt_thr) > top_k)
        def _():
            next_f = float_of(thr + 1)
            need = (top_k - count(lambda sc: sc >= next_f)).astype(F32)
            seen = jnp.zeros((1, Q_BLOCK), F32)
            for k0 in chunks:
                sc = score_s[k0:k0 + kc_rows, :]
                eq = (sc >= thr_f) & (sc < next_f)
                eq_f = jnp.where(eq, 1.0, 0.0)
                rank = _dot(tri_s[...], eq_f.astype(BF16)) + seen
                score_s[k0:k0 + kc_rows, :] = jnp.where(eq & (rank >= need), -jnp.inf, sc)
                seen = seen + jnp.sum(eq_f, axis=0, keepdims=True)

        m_acc = jnp.full((SUBLANES_V7X, wide), -jnp.inf, F32)
        for k0 in chunks:
            s_all = _dot_nt(ka_ref[0, k0:k0 + kc_rows, :], qa_s[...])
            sel = score_s[k0:k0 + kc_rows, :] >= thr_f
            s = jnp.concatenate(
                [jnp.where(sel, s_all[:, h * Q_BLOCK:(h + 1) * Q_BLOCK], -jnp.inf) for h in range(N_HEADS)],
                axis=1)
            s_scr[k0:k0 + kc_rows, :] = s
            m_acc = jnp.maximum(m_acc, s.reshape(sub_groups, SUBLANES_V7X, wide).max(axis=0))
        m_fin = jnp.max(m_acc, axis=0, keepdims=True)

        l_acc = jnp.zeros((SUBLANES_V7X, wide), F32)
        o_t = jnp.zeros((HEAD_DIM, wide), F32)
        for j, k0 in enumerate(chunks):
            p = jnp.exp2(s_scr[k0:k0 + kc_rows, :] - m_fin)
            o_t = o_t + _dot(vt_ref[0, j], p.astype(BF16))
            l_acc = l_acc + p.reshape(sub_groups, SUBLANES_V7X, wide).sum(axis=0)
        acc_s[...] = o_t / jnp.sum(l_acc, axis=0, keepdims=True)

    def q_block(qb, carry):
        r0 = pl.multiple_of(qb * Q_BLOCK, Q_BLOCK)
        qi = qi_ref[0, pl.ds(r0, Q_BLOCK), :]
        qa = qa_ref[0, pl.ds(r0, Q_BLOCK), :]
        for h in range(N_HEADS):
            qi_s[h * Q_BLOCK:(h + 1) * Q_BLOCK, :] = qi[:, h * HEAD_DIM:(h + 1) * HEAD_DIM]
            qa_s[h * Q_BLOCK:(h + 1) * Q_BLOCK, :] = qa[:, h * HEAD_DIM:(h + 1) * HEAD_DIM]
        n_chunks = (r0 + Q_BLOCK + kc_rows - 1) // kc_rows
        for n in range(1, seq // kc_rows + 1):
            pl.when(n_chunks == n)(functools.partial(attend, qb, n))
        o_t = acc_s[...]
        outs = [o_t[:, h * Q_BLOCK:(h + 1) * Q_BLOCK].T for h in range(N_HEADS)]
        o_ref[0, pl.ds(r0, Q_BLOCK), :] = jnp.concatenate(outs, axis=1).astype(BF16)
        return carry

    lax.fori_loop(0, n_qb, q_block, 0)


def _sparse_attention(qi, qa, w_t, ki, ka, v_t, top_k):
    b, seq, w = qi.shape
    batch3 = lambda s1, s2: pl.BlockSpec((1, s1, s2), lambda i: (i, 0, 0))
    batch4 = lambda s1, s2, s3: pl.BlockSpec((1, s1, s2, s3), lambda i: (i, 0, 0, 0))
    return pl.pallas_call(
        functools.partial(_sparse_kernel, top_k=top_k),
        grid=(b,),
        in_specs=[
            batch3(seq, w), batch3(seq, w),
            batch4(seq // Q_BLOCK, N_HEADS, Q_BLOCK),
            batch3(seq, HEAD_DIM), batch3(seq, HEAD_DIM),
            batch4(seq // KEY_CHUNK, HEAD_DIM, KEY_CHUNK),
        ],
        out_specs=batch3(seq, w),
        out_shape=jax.ShapeDtypeStruct((b, seq, w), BF16),
        scratch_shapes=[
            pltpu.VMEM((N_HEADS * Q_BLOCK, HEAD_DIM), BF16),
            pltpu.VMEM((N_HEADS * Q_BLOCK, HEAD_DIM), BF16),
            pltpu.VMEM((seq, Q_BLOCK), F32),
            pltpu.VMEM((seq, N_HEADS * Q_BLOCK), F32),
            pltpu.VMEM((HEAD_DIM, N_HEADS * Q_BLOCK), F32),
            pltpu.VMEM((KEY_CHUNK, KEY_CHUNK), BF16),
        ],
        compiler_params=pltpu.CompilerParams(
            dimension_semantics=("arbitrary",), vmem_limit_bytes=VMEM_LIMIT),
        name="sparse_attention",
    )(qi, qa, w_t, ki, ka, v_t)


BAND_Q = 256
BAND_KEYS = N_PAST_CHUNKS * CHUNK + BAND_Q
BAND_PIECES = BAND_KEYS // BAND_Q
BIAS_ROWS_UNROLL = 8


def _band_kernel(q_ref, k_ref, vt_ref, bias_ref, o_ref, s_a, s_b):
    seq = q_ref.shape[2]
    n_items = (seq // BAND_Q) * HEAD_PAIRS
    groups = BAND_Q // SUBLANES_V7X
    lane = lax.broadcasted_iota(I32, (1, LANES_V7X), 1)
    head_lanes = (lane < HEAD_DIM, lane >= HEAD_DIM)

    def pieces_of(qb):
        out = []
        for w in range(BAND_PIECES):
            kb_idx = qb - (BAND_PIECES - 1) + w
            kb_c = jnp.maximum(kb_idx, 0)
            out.append((kb_c, pl.multiple_of(kb_c * BAND_Q, BAND_Q),
                        jnp.where(kb_idx >= 0, 0.0, NEG_BIG)))
        return out

    def logits(item, s_scr):
        qb, pr = item // HEAD_PAIRS, item % HEAD_PAIRS
        r0 = pl.multiple_of(qb * BAND_Q, BAND_Q)
        q2 = q_ref[pr, 0, pl.ds(r0, BAND_Q), :]
        for e in range(2):
            q_h = jnp.where(head_lanes[e], q2, jnp.zeros_like(q2))
            for w, (_, row, _) in enumerate(pieces_of(qb)):
                rows = slice(w * BAND_Q, (w + 1) * BAND_Q)
                k2 = k_ref[pr, 0, pl.ds(row, BAND_Q), :]
                s_scr[e, rows, :] = _dot_nt(k2, q_h) + bias_ref[2 * pr + e, rows, :]

    def softmax_values(item, s_scr):
        qb, pr = item // HEAD_PAIRS, item % HEAD_PAIRS
        r0 = pl.multiple_of(qb * BAND_Q, BAND_Q)
        pieces = pieces_of(qb)
        o_pair = []
        for e in range(2):
            m_acc = jnp.full((SUBLANES_V7X, BAND_Q), -jnp.inf, F32)
            for w, (_, _, pen) in enumerate(pieces):
                rows = slice(w * BAND_Q, (w + 1) * BAND_Q)
                m_acc = jnp.maximum(
                    m_acc, s_scr[e, rows, :].reshape(groups, SUBLANES_V7X, BAND_Q).max(axis=0) + pen)
            m_fin = jnp.max(m_acc, axis=0, keepdims=True)
            l_acc = jnp.zeros((SUBLANES_V7X, BAND_Q), F32)
            o_t = jnp.zeros((HEAD_DIM, BAND_Q), F32)
            for w, (kb_c, _, pen) in enumerate(pieces):
                rows = slice(w * BAND_Q, (w + 1) * BAND_Q)
                p = jnp.exp2(s_scr[e, rows, :] - (m_fin - pen))
                l_acc = l_acc + p.reshape(groups, SUBLANES_V7X, BAND_Q).sum(axis=0)
                v_h = vt_ref[0, pr, kb_c][e * HEAD_DIM:(e + 1) * HEAD_DIM, :]
                o_t = o_t + _dot(v_h, p.astype(BF16))
            o_pair.append(o_t / jnp.sum(l_acc, axis=0, keepdims=True))
        o_ref[pr, 0, pl.ds(r0, BAND_Q), :] = jnp.concatenate(o_pair, axis=0).T.astype(BF16)

    logits(0, s_a)

    def step(u, c):
        logits(2 * u + 1, s_b)
        softmax_values(2 * u, s_a)
        logits(2 * u + 2, s_a)
        softmax_values(2 * u + 1, s_b)
        return c

    lax.fori_loop(0, n_items // 2 - 1, step, 0)
    logits(n_items - 1, s_b)
    softmax_values(n_items - 2, s_a)
    softmax_values(n_items - 1, s_b)


def _band_attention(qb_pm, kb_pm, vb_t, bias_t):
    pairs, b, seq, lanes = qb_pm.shape
    batch4 = pl.BlockSpec((pairs, 1, seq, lanes), lambda i: (0, i, 0, 0))
    return pl.pallas_call(
        _band_kernel,
        grid=(b,),
        in_specs=[batch4, batch4,
                  pl.BlockSpec((1,) + vb_t.shape[1:], lambda i: (i, 0, 0, 0, 0)),
                  _const_spec(bias_t.shape)],
        out_specs=batch4,
        out_shape=jax.ShapeDtypeStruct(qb_pm.shape, BF16),
        scratch_shapes=[pltpu.VMEM((2, BAND_KEYS, BAND_Q), F32)] * 2,
        compiler_params=pltpu.CompilerParams(
            dimension_semantics=("arbitrary",), vmem_limit_bytes=VMEM_LIMIT),
        name="band_attention",
    )(qb_pm, kb_pm, vb_t, bias_t)


def _bias_table_kernel(vec_ref, o_ref):
    period = vec_ref.shape[2]
    base = pltpu.roll(jnp.broadcast_to(vec_ref[0], (SUBLANES_V7X, period)), 0, 1, stride=1, stride_axis=0)
    q_chunk = lax.broadcasted_iota(I32, (SUBLANES_V7X, BAND_Q), 1) >> CHUNK_SHIFT

    def rows(c, carry):
        for u in range(BIAS_ROWS_UNROLL):
            j0 = pl.multiple_of((c * BIAS_ROWS_UNROLL + u) * SUBLANES_V7X, SUBLANES_V7X)
            blk = pltpu.roll(base, j0, 1)[:, :BAND_Q]
            k_chunk = j0 >> CHUNK_SHIFT
            in_band = (q_chunk <= k_chunk) & (q_chunk + N_PAST_CHUNKS >= k_chunk)
            o_ref[0, pl.ds(j0, SUBLANES_V7X), :] = jnp.where(in_band, blk * LOG2_E, NEG_BIG)
        return carry

    lax.fori_loop(0, BAND_KEYS // (SUBLANES_V7X * BIAS_ROWS_UNROLL), rows, 0)


def _band_bias_t(rel_bias_l):
    n_rel = REL_HI - REL_LO + 1
    period = BAND_KEYS + BAND_Q
    pad = N_PAST_CHUNKS * CHUNK
    lo_pos = period - pad + REL_LO
    rb = rel_bias_l.astype(F32)
    hi_val, lo_val = rb[:, n_rel - 1:n_rel], rb[:, 0:1]
    vec = jnp.concatenate([
        jnp.broadcast_to(hi_val, (N_HEADS, BAND_Q + 1)),
        jnp.broadcast_to(lo_val, (N_HEADS, lo_pos - (BAND_Q + 1))),
        rb,
        jnp.broadcast_to(hi_val, (N_HEADS, period - lo_pos - n_rel)),
    ], axis=1)
    assert vec.shape[1] == period and period % LANES_V7X == 0
    return pl.pallas_call(
        _bias_table_kernel,
        grid=(N_HEADS,),
        in_specs=[pl.BlockSpec((1, 1, period), lambda h: (h, 0, 0))],
        out_specs=pl.BlockSpec((1, BAND_KEYS, BAND_Q), lambda h: (h, 0, 0)),
        out_shape=jax.ShapeDtypeStruct((N_HEADS, BAND_KEYS, BAND_Q), F32),
        name="band_bias_table",
    )(vec.reshape(N_HEADS, 1, period))


def _out_ffn_kernel(x_ref, a_ref, b_ref, ga_ref, gb_ref, mod_ref, woa_ref, wob_ref, wout_ref,
                    g_ref, wgu_ref, wdown_ref, o_ref):
    d_ff = wdown_ref.shape[0]
    y_a = _dot(a_ref[...], woa_ref[...])
    attn_b = jnp.concatenate([b_ref[pr] for pr in range(HEAD_PAIRS)], axis=1)
    y_b = _dot(attn_b, wob_ref[...])
    merged = ga_ref[...].astype(F32) * y_a + gb_ref[...].astype(F32) * y_b
    x1 = x_ref[...] + mod_ref[0, 2:3, :] * _dot(merged.astype(BF16), wout_ref[...])

    y = x1 * lax.rsqrt(jnp.mean(x1 * x1, axis=-1, keepdims=True) + EPS) * g_ref[...]
    h2 = y * (1.0 + mod_ref[0, 4:5, :]) + mod_ref[0, 3:4, :]
    gu = _dot(h2.astype(BF16), wgu_ref[...])
    gate, up = gu[:, 0:d_ff], gu[:, d_ff:2 * d_ff]
    act = gate * jax.nn.sigmoid(gate) * up
    o_ref[...] = x1 + mod_ref[0, 5:6, :] * _dot(act.astype(BF16), wdown_ref[...])


def _out_ffn(xf, attn_a, attn_b, ga, gb, mod_l, w_oa, w_ob, w_out, g_ffn, w_gu, w_down, seq):
    t, d = xf.shape
    tm = ROW_TILE
    tiles_per_batch = seq // tm
    row = lambda width: pl.BlockSpec((tm, width), lambda i: (i, 0))
    return pl.pallas_call(
        _out_ffn_kernel,
        grid=(t // tm,),
        in_specs=[
            row(d), row(W_HEADS), pl.BlockSpec((HEAD_PAIRS, tm, LANES_V7X), lambda i: (0, i, 0)), row(d), row(d),
            pl.BlockSpec((1, 6, d), lambda i: (i // tiles_per_batch, 0, 0)),
            _const_spec(w_oa.shape), _const_spec(w_ob.shape), _const_spec(w_out.shape),
            _const_spec((1, d)), _const_spec(w_gu.shape), _const_spec(w_down.shape),
        ],
        out_specs=row(d),
        out_shape=jax.ShapeDtypeStruct((t, d), F32),
        compiler_params=pltpu.CompilerParams(
            dimension_semantics=("arbitrary",), vmem_limit_bytes=VMEM_LIMIT_FFN),
        name="out_proj_ffn",
    )(xf, attn_a, attn_b, ga, gb, mod_l, w_oa, w_ob, w_out, g_ffn, w_gu, w_down)


def _rope_tables(positions):
    half = ROT_DIM // 2
    inv_freq = ROPE_THETA ** (-jnp.arange(0, ROT_DIM, 2, dtype=F32) / ROT_DIM)
    ang = positions.astype(F32).reshape(-1, 1) * inv_freq
    cos, sin = jnp.cos(ang), jnp.sin(ang)
    t = cos.shape[0]
    ones = jnp.ones((t, HEAD_DIM - ROT_DIM), F32)
    zeros = jnp.zeros((t, HEAD_DIM - half), F32)
    tab_c = jnp.concatenate([cos, cos, ones], axis=1)
    tab_s1 = jnp.concatenate([-sin, zeros], axis=1)
    tab_s2 = jnp.concatenate([jnp.zeros((t, half), F32), sin, zeros[:, half:]], axis=1)
    rep = LANES_V7X // HEAD_DIM
    return tuple(jnp.tile(tb, (1, rep)) for tb in (tab_c, tab_s1, tab_s2))


def _pack_in_projection(w_in_l, b_in_l, d):
    sizes = (W_HEADS, HEAD_DIM, HEAD_DIM, W_HEADS, HEAD_DIM, N_HEADS, W_HEADS, W_HEADS, W_HEADS, d, d)
    offs = [0]
    for s in sizes:
        offs.append(offs[-1] + s)
    seg = lambda a, i: a[..., offs[i]:offs[i + 1]]
    order = (0, 3, 6, 7, 8, 9, 10, 1, 2, 4, 5)
    pad = SMALL_W - (3 * HEAD_DIM + N_HEADS)
    w_bf = w_in_l.astype(BF16)
    w_big = jnp.concatenate([seg(w_bf, i) for i in order] + [jnp.zeros((d, pad), BF16)], axis=1)
    b_big = jnp.concatenate([seg(b_in_l, i) for i in order] + [jnp.zeros((pad,), b_in_l.dtype)], axis=0)
    return w_big, b_big.reshape(1, -1).astype(F32)


def kernel(x, c, positions, w_ada, b_ada, g_mix, w_in, b_in, qn_a, kn_a, qn_b, kn_b, rel_bias,
           w_oa, w_ob, w_out, g_ffn, w_gu, w_down):
    b, seq, d = x.shape
    depth = w_ada.shape[0]
    t = b * seq
    top_k = min(TOPK_MAX, seq // 4)
    assert seq % KEY_CHUNK == 0 and seq % ROW_TILE == 0 and seq % BAND_Q == 0 and d % LANES_V7X == 0

    mod = _modulation(c, w_ada, b_ada).reshape(depth, b, 6, d)
    tabs = _rope_tables(positions)
    group = jnp.arange(W_HEADS) // HEAD_DIM
    gmat = jnp.where(group[:, None] == group[None, :], 1.0 / HEAD_DIM, 0.0).astype(BF16)
    lane = jnp.arange(LANES_V7X)
    tile_heads = lambda g: jnp.tile(g.astype(F32), N_HEADS)

    xf = x.reshape(t, d)
    for l in range(depth):
        w_big, b_big = _pack_in_projection(w_in[l], b_in[l], d)
        q_scale = ATTN_SCALE * LOG2_E
        gains = jnp.stack([tile_heads(qn_a[l]) * q_scale, tile_heads(qn_b[l]) * q_scale, tile_heads(kn_b[l])])
        sgains = jnp.stack([
            jnp.concatenate([kn_a[l].astype(F32), jnp.ones((LANES_V7X - HEAD_DIM,), F32)]),
            jnp.where((lane >= HEAD_DIM) & (lane < HEAD_DIM + N_HEADS), INDEX_SCALE, 1.0).astype(F32)])
        qa, qi, qb, kb, vb, ga, gb, sm = _in_projection(
            xf, mod[l], g_mix[l].reshape(1, d), w_big, b_big, gmat, tabs, gains, sgains, seq)

        ka = sm[:, 0:HEAD_DIM].astype(BF16).reshape(b, seq, HEAD_DIM)
        va_t = sm[:, HEAD_DIM:2 * HEAD_DIM].astype(BF16).reshape(b, seq // KEY_CHUNK, KEY_CHUNK, HEAD_DIM)
        va_t = va_t.transpose(0, 1, 3, 2)
        ki = sm[:, 2 * HEAD_DIM:3 * HEAD_DIM].astype(BF16).reshape(b, seq, HEAD_DIM)
        w_t = sm[:, 3 * HEAD_DIM:3 * HEAD_DIM + N_HEADS].reshape(b, seq // Q_BLOCK, Q_BLOCK, N_HEADS)
        w_t = w_t.transpose(0, 1, 3, 2)

        to3 = lambda a: a.reshape(b, seq, W_HEADS)
        attn_a = _sparse_attention(to3(qi), to3(qa), w_t, ki, ka, va_t, top_k)
        pm4 = lambda a: a.reshape(HEAD_PAIRS, b, seq, LANES_V7X)
        vb_t = vb.reshape(HEAD_PAIRS, b, seq // BAND_Q, BAND_Q, LANES_V7X).transpose(1, 0, 2, 4, 3)
        attn_b = _band_attention(pm4(qb), pm4(kb), vb_t, _band_bias_t(rel_bias[l]))

        xf = _out_ffn(xf, attn_a.reshape(t, W_HEADS), attn_b.reshape(HEAD_PAIRS, t, LANES_V7X), ga, gb, mod[l],
                      w_oa[l].astype(BF16), w_ob[l].astype(BF16), w_out[l].astype(BF16),
                      g_ffn[l].reshape(1, d), w_gu[l].astype(BF16), w_down[l].astype(BF16), seq)
    return xf.reshape(b, seq, d)
```

```python
import functools

import jax
import jax.numpy as jnp
from jax import lax
from jax.experimental import pallas as pl
from jax.experimental.pallas import tpu as pltpu

F32 = jnp.float32
BF16 = jnp.bfloat16
I32 = jnp.int32

CHUNK = 64
CHUNK_SHIFT = CHUNK.bit_length() - 1
HEAD_DIM = 64
N_HEADS = 8
TOPK_MAX = 256
N_PAST_CHUNKS = 8
REL_LO = -(CHUNK - 1)
REL_HI = 256
ROT_DIM = HEAD_DIM // 4
ROPE_THETA = 500000.0
EPS = 1e-6
W_HEADS = N_HEADS * HEAD_DIM
INDEX_SCALE = (HEAD_DIM ** -0.5) * (N_HEADS ** -0.5)
ATTN_SCALE = HEAD_DIM ** -0.5
LOG2_E = 1.4426950408889634
HEAD_PAIRS = N_HEADS // 2

LANES_V7X = 128
SUBLANES_V7X = 8
VMEM_BYTES_V7X = 64 * 1024 * 1024
VMEM_LIMIT = VMEM_BYTES_V7X * 3 // 4
VMEM_LIMIT_FFN = VMEM_BYTES_V7X * 7 // 8

Q_BLOCK = 128
KEY_CHUNK = 512
SMALL_W = 256
ROW_TILE = 256
INT_MIN = -(2 ** 31)
RANK_LOWEST_FINITE = INT_MIN + (1 << 23)
NEG_BIG = -1e30


def _dot(a, b):
    return jnp.dot(a, b, preferred_element_type=F32)


def _dot_nt(a, b):
    return lax.dot_general(a, b, (((1,), (1,)), ((), ())), preferred_element_type=F32)


def _mod_kernel(c_ref, w_ref, b_ref, o_ref):
    c = c_ref[...]
    c_act = (c * jax.nn.sigmoid(c)).astype(BF16)
    o_ref[0] = _dot(c_act, w_ref[0].astype(BF16)) + b_ref[0]


def _modulation(c, w_ada, b_ada):
    depth, d, n = w_ada.shape
    b = c.shape[0]
    tn = n // 6
    return pl.pallas_call(
        _mod_kernel,
        grid=(depth, n // tn),
        in_specs=[
            pl.BlockSpec((b, d), lambda l, j: (0, 0)),
            pl.BlockSpec((1, d, tn), lambda l, j: (l, 0, j)),
            pl.BlockSpec((1, 1, tn), lambda l, j: (l, 0, j)),
        ],
        out_specs=pl.BlockSpec((1, b, tn), lambda l, j: (l, 0, j)),
        out_shape=jax.ShapeDtypeStruct((depth, b, n), F32),
        name="adaln_modulation",
    )(c, w_ada, b_ada.reshape(depth, 1, n))


def _group_mean_sq(v, gmat):
    return _dot((v * v).astype(BF16), gmat)


def _rope_cols(v, tab_c, tab_s1, tab_s2):
    cols = []
    for c0 in range(0, v.shape[1], LANES_V7X):
        vc = v[:, c0:c0 + LANES_V7X]
        fwd = pltpu.roll(vc, LANES_V7X - ROT_DIM // 2, axis=1)
        bwd = pltpu.roll(vc, ROT_DIM // 2, axis=1)
        cols.append(vc * tab_c + fwd * tab_s1 + bwd * tab_s2)
    return cols[0] if len(cols) == 1 else jnp.concatenate(cols, axis=1)


def _inproj_kernel(x_ref, mod_ref, g_ref, w_ref, b_ref, gmat_ref, tc_ref, ts1_ref, ts2_ref,
                   gain_ref, sgain_ref,
                   qa_ref, qi_ref, qb_ref, kb_ref, vbt_ref, ga_ref, gb_ref, ka_ref, ki_ref, vat_ref, wt_ref):
    d = x_ref.shape[1]
    x = x_ref[...]
    y = x * lax.rsqrt(jnp.mean(x * x, axis=-1, keepdims=True) + EPS) * g_ref[...]
    h = y * (1.0 + mod_ref[0, 1:2, :]) + mod_ref[0, 0:1, :]
    z = _dot(h.astype(BF16), w_ref[...]) + b_ref[...]

    gmat = gmat_ref[...]
    tab_c, tab_s1, tab_s2 = tc_ref[...], ts1_ref[...], ts2_ref[...]
    w = W_HEADS

    def head_norm(v, gain):
        return v * lax.rsqrt(_group_mean_sq(v, gmat) + EPS) * gain

    qa = head_norm(z[:, 0:w], gain_ref[0:1, :])
    qa_ref[...] = _rope_cols(qa, tab_c, tab_s1, tab_s2).astype(BF16)
    qi_ref[...] = _rope_cols(z[:, w:2 * w], tab_c, tab_s1, tab_s2).astype(BF16)
    qb = head_norm(z[:, 2 * w:3 * w], gain_ref[1:2, :]).astype(BF16)
    kb = head_norm(z[:, 3 * w:4 * w], gain_ref[2:3, :]).astype(BF16)
    for pr in range(HEAD_PAIRS):
        cols = slice(pr * LANES_V7X, (pr + 1) * LANES_V7X)
        qb_ref[pr] = qb[:, cols]
        kb_ref[pr] = kb[:, cols]
        vbt_ref[pr, 0] = z[:, 4 * w + pr * LANES_V7X:4 * w + (pr + 1) * LANES_V7X].T.astype(BF16)
    ga_ref[...] = jax.nn.sigmoid(z[:, 5 * w:5 * w + d]).astype(BF16)
    gb_ref[...] = jax.nn.sigmoid(z[:, 5 * w + d:5 * w + 2 * d]).astype(BF16)

    sm = z[:, 5 * w + 2 * d:5 * w + 2 * d + SMALL_W]
    lane = lax.broadcasted_iota(I32, (1, LANES_V7X), 1)
    first_head = lane < HEAD_DIM
    c0 = sm[:, 0:LANES_V7X]
    c1 = sm[:, LANES_V7X:2 * LANES_V7X]
    ms0 = _group_mean_sq(c0, gmat[0:LANES_V7X, 0:LANES_V7X])
    c0 = jnp.where(first_head, c0 * lax.rsqrt(ms0 + EPS) * sgain_ref[0:1, :], c0)
    c1 = c1 * sgain_ref[1:2, :]
    t_c = jnp.where(first_head, tab_c, 1.0)
    t_s1 = jnp.where(first_head, tab_s1, 0.0)
    t_s2 = jnp.where(first_head, tab_s2, 0.0)
    c0 = _rope_cols(c0, t_c, t_s1, t_s2)
    c1 = _rope_cols(c1, t_c, t_s1, t_s2)
    ka_ref[...] = jnp.concatenate([c0[:, 0:HEAD_DIM]] * 2, axis=1).astype(BF16)
    ki_ref[...] = jnp.concatenate([c1[:, 0:HEAD_DIM]] * 2, axis=1).astype(BF16)
    vat_ref[...] = c0.T[HEAD_DIM:2 * HEAD_DIM, :].astype(BF16)
    w_t = c1.T[HEAD_DIM:HEAD_DIM + N_HEADS, :]
    for blk in range(wt_ref.shape[0]):
        wt_ref[blk] = w_t[:, blk * Q_BLOCK:(blk + 1) * Q_BLOCK]


def _const_spec(shape):
    nd = len(shape)
    return pl.BlockSpec(shape, lambda i: (0,) * nd, pipeline_mode=pl.Buffered(1))


def _in_projection(xf, mod_l, g_mix, w_big, b_big, gmat, tabs, gains, sgains, seq):
    t, d = xf.shape
    tm = ROW_TILE
    tiles_per_batch = seq // tm
    n_big = w_big.shape[1]
    row = lambda width: pl.BlockSpec((tm, width), lambda i: (i, 0))
    pair_major = pl.BlockSpec((HEAD_PAIRS, tm, LANES_V7X), lambda i: (0, i, 0))
    pm_shape = jax.ShapeDtypeStruct((HEAD_PAIRS, t, LANES_V7X), BF16)
    assert tm == BAND_Q and tm % Q_BLOCK == 0
    vbt_spec = pl.BlockSpec((HEAD_PAIRS, 1, LANES_V7X, tm), lambda i: (0, i, 0, 0))
    vbt_shape = jax.ShapeDtypeStruct((HEAD_PAIRS, t // tm, LANES_V7X, tm), BF16)
    return pl.pallas_call(
        _inproj_kernel,
        grid=(t // tm,),
        in_specs=[
            row(d),
            pl.BlockSpec((1, 6, d), lambda i: (i // tiles_per_batch, 0, 0)),
            _const_spec((1, d)),
            _const_spec((d, n_big)),
            _const_spec((1, n_big)),
            _const_spec((W_HEADS, W_HEADS)),
            row(LANES_V7X), row(LANES_V7X), row(LANES_V7X),
            _const_spec((3, W_HEADS)),
            _const_spec((2, LANES_V7X)),
        ],
        out_specs=[row(W_HEADS), row(W_HEADS), pair_major, pair_major, vbt_spec, row(d), row(d),
                   row(LANES_V7X), row(LANES_V7X),
                   pl.BlockSpec((HEAD_DIM, tm), lambda i: (0, i)),
                   pl.BlockSpec((tm // Q_BLOCK, N_HEADS, Q_BLOCK), lambda i: (i, 0, 0))],
        out_shape=[jax.ShapeDtypeStruct((t, W_HEADS), BF16)] * 2 + [pm_shape] * 2 + [vbt_shape]
        + [jax.ShapeDtypeStruct((t, d), BF16)] * 2 + [jax.ShapeDtypeStruct((t, LANES_V7X), BF16)] * 2
        + [jax.ShapeDtypeStruct((HEAD_DIM, t), BF16), jax.ShapeDtypeStruct((t // Q_BLOCK, N_HEADS, Q_BLOCK), F32)],
        compiler_params=pltpu.CompilerParams(
            dimension_semantics=("arbitrary",), vmem_limit_bytes=VMEM_LIMIT),
        name="in_projection",
    )(xf, mod_l, g_mix, w_big, b_big, gmat, *tabs, gains, sgains)


def _sparse_kernel(qi_ref, qa_ref, wt_ref, ki_ref, ka_ref, vt_ref, o_ref,
                   qi_s, qa_s, score_s, s_scr, acc_s, tri_s, *, top_k):
    seq = qi_ref.shape[1]
    n_qb = seq // Q_BLOCK
    kc_rows = KEY_CHUNK
    sub_groups = kc_rows // SUBLANES_V7X
    wide = N_HEADS * Q_BLOCK
    r_i = lax.broadcasted_iota(I32, (kc_rows, kc_rows), 0)
    c_i = lax.broadcasted_iota(I32, (kc_rows, kc_rows), 1)
    tri_s[...] = jnp.where(c_i < r_i, 1.0, 0.0).astype(BF16)
    lane128 = lax.broadcasted_iota(I32, (1, LANES_V7X), 1)
    head_lanes = (lane128 < HEAD_DIM, lane128 >= HEAD_DIM)

    def float_of(key):
        return jnp.broadcast_to(
            lax.bitcast_convert_type(key ^ ((key >> 31) & 0x7FFFFFFF), F32), (kc_rows, Q_BLOCK))

    def attend(qb, n_chunks):
        r0 = pl.multiple_of(qb * Q_BLOCK, Q_BLOCK)
        chunks = [j * kc_rows for j in range(n_chunks)]
        w_idx = wt_ref[qb]
        lane = lax.broadcasted_iota(I32, (1, Q_BLOCK), 1)
        limit = (((r0 + lane) >> CHUNK_SHIFT) + 1) * CHUNK
        row_id = lax.broadcasted_iota(I32, (kc_rows, Q_BLOCK), 0)

        for k0 in chunks:
            logits = _dot_nt(ki_ref[0, k0:k0 + kc_rows, :], qi_s[...])
            score = jnp.zeros((kc_rows, Q_BLOCK), F32)
            for h in range(N_HEADS):
                score = score + jnp.maximum(logits[:, h * Q_BLOCK:(h + 1) * Q_BLOCK], 0.0) * w_idx[h:h + 1, :]
            if k0 + kc_rows > (n_chunks - 1) * kc_rows:
                score = jnp.where(row_id < limit - k0, score, -jnp.inf)
            score_s[k0:k0 + kc_rows, :] = score

        def count(pred_fn):
            acc = jnp.zeros((SUBLANES_V7X, Q_BLOCK), I32)
            for k0 in chunks:
                hit = jnp.where(pred_fn(score_s[k0:k0 + kc_rows, :]), 1, 0).astype(I32)
                acc = acc + hit.reshape(sub_groups, SUBLANES_V7X, Q_BLOCK).sum(axis=0)
            return acc.sum(axis=0, keepdims=True)

        def search_bit(i, state):
            thr, cnt_thr = state
            cand = thr + jnp.left_shift(jnp.int32(1), 31 - i)
            cand_f = float_of(cand)
            cnt = count(lambda sc: sc >= cand_f)
            take = cnt >= top_k
            return jnp.where(take, cand, thr), jnp.where(take, cnt, cnt_thr)

        thr0 = jnp.full((1, Q_BLOCK), INT_MIN, I32)
        cnt0 = jnp.full((1, Q_BLOCK), n_chunks * kc_rows, I32)
        need_search = r0 + Q_BLOCK > top_k
        thr, cnt_thr = lax.fori_loop(0, jnp.where(need_search, 32, 0), search_bit, (thr0, cnt0))
        cnt_thr = jnp.where(need_search, cnt_thr, top_k)
        thr = jnp.maximum(thr, RANK_LOWEST_FINITE)
        thr_f = float_of(thr)

        @pl.when(jnp.max(cnt_thr) > top_k)
        def _():
            next_f = float_of(thr + 1)
            need = (top_k - count(lambda sc: sc >= next_f)).astype(F32)
            seen = jnp.zeros((1, Q_BLOCK), F32)
            for k0 in chunks:
                sc = score_s[k0:k0 + kc_rows, :]
                eq = (sc >= thr_f) & (sc < next_f)
                eq_f = jnp.where(eq, 1.0, 0.0)
                rank = _dot(tri_s[...], eq_f.astype(BF16)) + seen
                score_s[k0:k0 + kc_rows, :] = jnp.where(eq & (rank >= need), -jnp.inf, sc)
                seen = seen + jnp.sum(eq_f, axis=0, keepdims=True)

        m_acc = jnp.full((SUBLANES_V7X, wide), -jnp.inf, F32)
        for k0 in chunks:
            s_all = _dot_nt(ka_ref[0, k0:k0 + kc_rows, :], qa_s[...])
            sel = score_s[k0:k0 + kc_rows, :] >= thr_f
            s = jnp.concatenate(
                [jnp.where(sel, s_all[:, h * Q_BLOCK:(h + 1) * Q_BLOCK], -jnp.inf) for h in range(N_HEADS)],
                axis=1)
            s_scr[k0:k0 + kc_rows, :] = s
            m_acc = jnp.maximum(m_acc, s.reshape(sub_groups, SUBLANES_V7X, wide).max(axis=0))
        m_fin = jnp.max(m_acc, axis=0, keepdims=True)

        l_acc = jnp.zeros((SUBLANES_V7X, wide), F32)
        o_t = jnp.zeros((HEAD_DIM, wide), F32)
        for k0 in chunks:
            p = jnp.exp2(s_scr[k0:k0 + kc_rows, :] - m_fin)
            o_t = o_t + _dot(vt_ref[:, k0:k0 + kc_rows], p.astype(BF16))
            l_acc = l_acc + p.reshape(sub_groups, SUBLANES_V7X, wide).sum(axis=0)
        acc_s[...] = o_t / jnp.sum(l_acc, axis=0, keepdims=True)

    def q_block(qb, carry):
        r0 = pl.multiple_of(qb * Q_BLOCK, Q_BLOCK)
        qi = qi_ref[0, pl.ds(r0, Q_BLOCK), :]
        qa = qa_ref[0, pl.ds(r0, Q_BLOCK), :]
        for h in range(N_HEADS):
            cols = slice((h // 2) * LANES_V7X, (h // 2 + 1) * LANES_V7X)
            qi_s[h * Q_BLOCK:(h + 1) * Q_BLOCK, :] = jnp.where(head_lanes[h % 2], qi[:, cols], jnp.zeros_like(qi[:, cols]))
            qa_s[h * Q_BLOCK:(h + 1) * Q_BLOCK, :] = jnp.where(head_lanes[h % 2], qa[:, cols], jnp.zeros_like(qa[:, cols]))
        n_chunks = (r0 + Q_BLOCK + kc_rows - 1) // kc_rows
        for n in range(1, seq // kc_rows + 1):
            pl.when(n_chunks == n)(functools.partial(attend, qb, n))
        o_t = acc_s[...]
        outs = [o_t[:, h * Q_BLOCK:(h + 1) * Q_BLOCK].T for h in range(N_HEADS)]
        o_ref[0, pl.ds(r0, Q_BLOCK), :] = jnp.concatenate(outs, axis=1).astype(BF16)
        return carry

    lax.fori_loop(0, n_qb, q_block, 0)


def _sparse_attention(qi, qa, w_t, ki, ka, v_t, top_k):
    b, seq, w = qi.shape
    batch3 = lambda s1, s2: pl.BlockSpec((1, s1, s2), lambda i: (i, 0, 0))
    return pl.pallas_call(
        functools.partial(_sparse_kernel, top_k=top_k),
        grid=(b,),
        in_specs=[
            batch3(seq, w), batch3(seq, w),
            pl.BlockSpec((seq // Q_BLOCK, N_HEADS, Q_BLOCK), lambda i: (i, 0, 0)),
            batch3(seq, LANES_V7X), batch3(seq, LANES_V7X),
            pl.BlockSpec((HEAD_DIM, seq), lambda i: (0, i)),
        ],
        out_specs=batch3(seq, w),
        out_shape=jax.ShapeDtypeStruct((b, seq, w), BF16),
        scratch_shapes=[
            pltpu.VMEM((N_HEADS * Q_BLOCK, LANES_V7X), BF16),
            pltpu.VMEM((N_HEADS * Q_BLOCK, LANES_V7X), BF16),
            pltpu.VMEM((seq, Q_BLOCK), F32),
            pltpu.VMEM((seq, N_HEADS * Q_BLOCK), F32),
            pltpu.VMEM((HEAD_DIM, N_HEADS * Q_BLOCK), F32),
            pltpu.VMEM((KEY_CHUNK, KEY_CHUNK), BF16),
        ],
        compiler_params=pltpu.CompilerParams(
            dimension_semantics=("arbitrary",), vmem_limit_bytes=VMEM_LIMIT),
        name="sparse_attention",
    )(qi, qa, w_t, ki, ka, v_t)


BAND_Q = 256
BAND_KEYS = N_PAST_CHUNKS * CHUNK + BAND_Q
BAND_PIECES = BAND_KEYS // BAND_Q
BIAS_ROWS_UNROLL = 8


def _band_kernel(q_ref, k_ref, vt_ref, bias_ref, o_ref, s_a, s_b):
    seq = q_ref.shape[2]
    n_items = (seq // BAND_Q) * HEAD_PAIRS
    groups = BAND_Q // SUBLANES_V7X
    lane = lax.broadcasted_iota(I32, (1, LANES_V7X), 1)
    head_lanes = (lane < HEAD_DIM, lane >= HEAD_DIM)

    def pieces_of(qb):
        out = []
        for w in range(BAND_PIECES):
            kb_idx = qb - (BAND_PIECES - 1) + w
            kb_c = jnp.maximum(kb_idx, 0)
            out.append((kb_c, pl.multiple_of(kb_c * BAND_Q, BAND_Q),
                        jnp.where(kb_idx >= 0, 0.0, NEG_BIG)))
        return out

    def logits(item, s_scr):
        qb, pr = item // HEAD_PAIRS, item % HEAD_PAIRS
        r0 = pl.multiple_of(qb * BAND_Q, BAND_Q)
        q2 = q_ref[pr, 0, pl.ds(r0, BAND_Q), :]
        for e in range(2):
            q_h = jnp.where(head_lanes[e], q2, jnp.zeros_like(q2))
            for w, (_, row, _) in enumerate(pieces_of(qb)):
                rows = slice(w * BAND_Q, (w + 1) * BAND_Q)
                k2 = k_ref[pr, 0, pl.ds(row, BAND_Q), :]
                s_scr[e, rows, :] = _dot_nt(k2, q_h) + bias_ref[2 * pr + e, rows, :]

    def softmax_values(item, s_scr):
        qb, pr = item // HEAD_PAIRS, item % HEAD_PAIRS
        r0 = pl.multiple_of(qb * BAND_Q, BAND_Q)
        pieces = pieces_of(qb)
        o_pair = []
        for e in range(2):
            m_acc = jnp.full((SUBLANES_V7X, BAND_Q), -jnp.inf, F32)
            for w, (_, _, pen) in enumerate(pieces):
                rows = slice(w * BAND_Q, (w + 1) * BAND_Q)
                m_acc = jnp.maximum(
                    m_acc, s_scr[e, rows, :].reshape(groups, SUBLANES_V7X, BAND_Q).max(axis=0) + pen)
            m_fin = jnp.max(m_acc, axis=0, keepdims=True)
            l_acc = jnp.zeros((SUBLANES_V7X, BAND_Q), F32)
            o_t = jnp.zeros((HEAD_DIM, BAND_Q), F32)
            for w, (kb_c, _, pen) in enumerate(pieces):
                rows = slice(w * BAND_Q, (w + 1) * BAND_Q)
                p = jnp.exp2(s_scr[e, rows, :] - (m_fin - pen))
                l_acc = l_acc + p.reshape(groups, SUBLANES_V7X, BAND_Q).sum(axis=0)
                v_h = vt_ref[pr, kb_c][e * HEAD_DIM:(e + 1) * HEAD_DIM, :]
                o_t = o_t + _dot(v_h, p.astype(BF16))
            o_pair.append(o_t / jnp.sum(l_acc, axis=0, keepdims=True))
        o_ref[pr, 0, pl.ds(r0, BAND_Q), :] = jnp.concatenate(o_pair, axis=0).T.astype(BF16)

    logits(0, s_a)

    def step(u, c):
        logits(2 * u + 1, s_b)
        softmax_values(2 * u, s_a)
        logits(2 * u + 2, s_a)
        softmax_values(2 * u + 1, s_b)
        return c

    lax.fori_loop(0, n_items // 2 - 1, step, 0)
    logits(n_items - 1, s_b)
    softmax_values(n_items - 2, s_a)
    softmax_values(n_items - 1, s_b)


def _band_attention(qb_pm, kb_pm, vb_t, bias_t):
    pairs, b, seq, lanes = qb_pm.shape
    batch4 = pl.BlockSpec((pairs, 1, seq, lanes), lambda i: (0, i, 0, 0))
    return pl.pallas_call(
        _band_kernel,
        grid=(b,),
        in_specs=[batch4, batch4,
                  pl.BlockSpec((pairs, seq // BAND_Q, lanes, BAND_Q), lambda i: (0, i, 0, 0)),
                  _const_spec(bias_t.shape)],
        out_specs=batch4,
        out_shape=jax.ShapeDtypeStruct(qb_pm.shape, BF16),
        scratch_shapes=[pltpu.VMEM((2, BAND_KEYS, BAND_Q), F32)] * 2,
        compiler_params=pltpu.CompilerParams(
            dimension_semantics=("arbitrary",), vmem_limit_bytes=VMEM_LIMIT),
        name="band_attention",
    )(qb_pm, kb_pm, vb_t, bias_t)


def _bias_table_kernel(vec_ref, o_ref):
    period = vec_ref.shape[2]
    base = pltpu.roll(jnp.broadcast_to(vec_ref[0], (SUBLANES_V7X, period)), 0, 1, stride=1, stride_axis=0)
    q_chunk = lax.broadcasted_iota(I32, (SUBLANES_V7X, BAND_Q), 1) >> CHUNK_SHIFT

    def rows(c, carry):
        for u in range(BIAS_ROWS_UNROLL):
            j0 = pl.multiple_of((c * BIAS_ROWS_UNROLL + u) * SUBLANES_V7X, SUBLANES_V7X)
            blk = pltpu.roll(base, j0, 1)[:, :BAND_Q]
            k_chunk = j0 >> CHUNK_SHIFT
            in_band = (q_chunk <= k_chunk) & (q_chunk + N_PAST_CHUNKS >= k_chunk)
            o_ref[0, pl.ds(j0, SUBLANES_V7X), :] = jnp.where(in_band, blk * LOG2_E, NEG_BIG)
        return carry

    lax.fori_loop(0, BAND_KEYS // (SUBLANES_V7X * BIAS_ROWS_UNROLL), rows, 0)


def _band_bias_t(rel_bias_l):
    n_rel = REL_HI - REL_LO + 1
    period = BAND_KEYS + BAND_Q
    pad = N_PAST_CHUNKS * CHUNK
    lo_pos = period - pad + REL_LO
    rb = rel_bias_l.astype(F32)
    hi_val, lo_val = rb[:, n_rel - 1:n_rel], rb[:, 0:1]
    vec = jnp.concatenate([
        jnp.broadcast_to(hi_val, (N_HEADS, BAND_Q + 1)),
        jnp.broadcast_to(lo_val, (N_HEADS, lo_pos - (BAND_Q + 1))),
        rb,
        jnp.broadcast_to(hi_val, (N_HEADS, period - lo_pos - n_rel)),
    ], axis=1)
    assert vec.shape[1] == period and period % LANES_V7X == 0
    return pl.pallas_call(
        _bias_table_kernel,
        grid=(N_HEADS,),
        in_specs=[pl.BlockSpec((1, 1, period), lambda h: (h, 0, 0))],
        out_specs=pl.BlockSpec((1, BAND_KEYS, BAND_Q), lambda h: (h, 0, 0)),
        out_shape=jax.ShapeDtypeStruct((N_HEADS, BAND_KEYS, BAND_Q), F32),
        name="band_bias_table",
    )(vec.reshape(N_HEADS, 1, period))


def _out_ffn_kernel(x_ref, a_ref, b_ref, ga_ref, gb_ref, mod_ref, woa_ref, wob_ref, wout_ref,
                    g_ref, wgu_ref, wdown_ref, o_ref):
    d_ff = wdown_ref.shape[0]
    y_a = _dot(a_ref[...], woa_ref[...])
    attn_b = jnp.concatenate([b_ref[pr] for pr in range(HEAD_PAIRS)], axis=1)
    y_b = _dot(attn_b, wob_ref[...])
    merged = ga_ref[...].astype(F32) * y_a + gb_ref[...].astype(F32) * y_b
    x1 = x_ref[...] + mod_ref[0, 2:3, :] * _dot(merged.astype(BF16), wout_ref[...])

    y = x1 * lax.rsqrt(jnp.mean(x1 * x1, axis=-1, keepdims=True) + EPS) * g_ref[...]
    h2 = y * (1.0 + mod_ref[0, 4:5, :]) + mod_ref[0, 3:4, :]
    gu = _dot(h2.astype(BF16), wgu_ref[...])
    gate, up = gu[:, 0:d_ff], gu[:, d_ff:2 * d_ff]
    act = gate * jax.nn.sigmoid(gate) * up
    o_ref[...] = x1 + mod_ref[0, 5:6, :] * _dot(act.astype(BF16), wdown_ref[...])


def _out_ffn(xf, attn_a, attn_b, ga, gb, mod_l, w_oa, w_ob, w_out, g_ffn, w_gu, w_down, seq):
    t, d = xf.shape
    tm = ROW_TILE
    tiles_per_batch = seq // tm
    row = lambda width: pl.BlockSpec((tm, width), lambda i: (i, 0))
    return pl.pallas_call(
        _out_ffn_kernel,
        grid=(t // tm,),
        in_specs=[
            row(d), row(W_HEADS), pl.BlockSpec((HEAD_PAIRS, tm, LANES_V7X), lambda i: (0, i, 0)), row(d), row(d),
            pl.BlockSpec((1, 6, d), lambda i: (i // tiles_per_batch, 0, 0)),
            _const_spec(w_oa.shape), _const_spec(w_ob.shape), _const_spec(w_out.shape),
            _const_spec((1, d)), _const_spec(w_gu.shape), _const_spec(w_down.shape),
        ],
        out_specs=row(d),
        out_shape=jax.ShapeDtypeStruct((t, d), F32),
        compiler_params=pltpu.CompilerParams(
            dimension_semantics=("arbitrary",), vmem_limit_bytes=VMEM_LIMIT_FFN),
        name="out_proj_ffn",
    )(xf, attn_a, attn_b, ga, gb, mod_l, w_oa, w_ob, w_out, g_ffn, w_gu, w_down)


def _rope_tables(positions):
    half = ROT_DIM // 2
    inv_freq = ROPE_THETA ** (-jnp.arange(0, ROT_DIM, 2, dtype=F32) / ROT_DIM)
    ang = positions.astype(F32).reshape(-1, 1) * inv_freq
    cos, sin = jnp.cos(ang), jnp.sin(ang)
    t = cos.shape[0]
    ones = jnp.ones((t, HEAD_DIM - ROT_DIM), F32)
    zeros = jnp.zeros((t, HEAD_DIM - half), F32)
    tab_c = jnp.concatenate([cos, cos, ones], axis=1)
    tab_s1 = jnp.concatenate([-sin, zeros], axis=1)
    tab_s2 = jnp.concatenate([jnp.zeros((t, half), F32), sin, zeros[:, half:]], axis=1)
    rep = LANES_V7X // HEAD_DIM
    return tuple(jnp.tile(tb, (1, rep)) for tb in (tab_c, tab_s1, tab_s2))


def _pack_in_projection(w_in_l, b_in_l, d):
    sizes = (W_HEADS, HEAD_DIM, HEAD_DIM, W_HEADS, HEAD_DIM, N_HEADS, W_HEADS, W_HEADS, W_HEADS, d, d)
    offs = [0]
    for s in sizes:
        offs.append(offs[-1] + s)
    seg = lambda a, i: a[..., offs[i]:offs[i + 1]]
    order = (0, 3, 6, 7, 8, 9, 10, 1, 2, 4, 5)
    pad = SMALL_W - (3 * HEAD_DIM + N_HEADS)
    w_bf = w_in_l.astype(BF16)
    w_big = jnp.concatenate([seg(w_bf, i) for i in order] + [jnp.zeros((d, pad), BF16)], axis=1)
    b_big = jnp.concatenate([seg(b_in_l, i) for i in order] + [jnp.zeros((pad,), b_in_l.dtype)], axis=0)
    return w_big, b_big.reshape(1, -1).astype(F32)


def kernel(x, c, positions, w_ada, b_ada, g_mix, w_in, b_in, qn_a, kn_a, qn_b, kn_b, rel_bias,
           w_oa, w_ob, w_out, g_ffn, w_gu, w_down):
    b, seq, d = x.shape
    depth = w_ada.shape[0]
    t = b * seq
    top_k = min(TOPK_MAX, seq // 4)
    assert seq % KEY_CHUNK == 0 and seq % ROW_TILE == 0 and seq % BAND_Q == 0 and d % LANES_V7X == 0

    mod = _modulation(c, w_ada, b_ada).reshape(depth, b, 6, d)
    tabs = _rope_tables(positions)
    group = jnp.arange(W_HEADS) // HEAD_DIM
    gmat = jnp.where(group[:, None] == group[None, :], 1.0 / HEAD_DIM, 0.0).astype(BF16)
    lane = jnp.arange(LANES_V7X)
    tile_heads = lambda g: jnp.tile(g.astype(F32), N_HEADS)

    xf = x.reshape(t, d)
    for l in range(depth):
        w_big, b_big = _pack_in_projection(w_in[l], b_in[l], d)
        q_scale = ATTN_SCALE * LOG2_E
        gains = jnp.stack([tile_heads(qn_a[l]) * q_scale, tile_heads(qn_b[l]) * q_scale, tile_heads(kn_b[l])])
        sgains = jnp.stack([
            jnp.concatenate([kn_a[l].astype(F32), jnp.ones((LANES_V7X - HEAD_DIM,), F32)]),
            jnp.where((lane >= HEAD_DIM) & (lane < HEAD_DIM + N_HEADS), INDEX_SCALE, 1.0).astype(F32)])
        qa, qi, qb, kb, vb_t, ga, gb, ka, ki, va_t, w_t = _in_projection(
            xf, mod[l], g_mix[l].reshape(1, d), w_big, b_big, gmat, tabs, gains, sgains, seq)

        to3 = lambda a: a.reshape(b, seq, a.shape[-1])
        attn_a = _sparse_attention(to3(qi), to3(qa), w_t, to3(ki), to3(ka), va_t, top_k)
        pm4 = lambda a: a.reshape(HEAD_PAIRS, b, seq, LANES_V7X)
        attn_b = _band_attention(pm4(qb), pm4(kb), vb_t, _band_bias_t(rel_bias[l]))

        xf = _out_ffn(xf, attn_a.reshape(t, W_HEADS), attn_b.reshape(HEAD_PAIRS, t, LANES_V7X), ga, gb, mod[l],
                      w_oa[l].astype(BF16), w_ob[l].astype(BF16), w_out[l].astype(BF16),
                      g_ffn[l].reshape(1, d), w_gu[l].astype(BF16), w_down[l].astype(BF16), seq)
    return xf.reshape(b, seq, d)
```

```python
import functools

import jax
import jax.numpy as jnp
from jax import lax
from jax.experimental import pallas as pl
from jax.experimental.pallas import tpu as pltpu

F32 = jnp.float32
BF16 = jnp.bfloat16
I32 = jnp.int32

CHUNK = 64
CHUNK_SHIFT = CHUNK.bit_length() - 1
HEAD_DIM = 64
N_HEADS = 8
TOPK_MAX = 256
N_PAST_CHUNKS = 8
REL_LO = -(CHUNK - 1)
REL_HI = 256
ROT_DIM = HEAD_DIM // 4
ROPE_THETA = 500000.0
EPS = 1e-6
W_HEADS = N_HEADS * HEAD_DIM
INDEX_SCALE = (HEAD_DIM ** -0.5) * (N_HEADS ** -0.5)
ATTN_SCALE = HEAD_DIM ** -0.5
LOG2_E = 1.4426950408889634
HEAD_PAIRS = N_HEADS // 2

LANES_V7X = 128
SUBLANES_V7X = 8
VMEM_BYTES_V7X = 64 * 1024 * 1024
VMEM_LIMIT = VMEM_BYTES_V7X * 3 // 4
VMEM_LIMIT_FFN = VMEM_BYTES_V7X * 7 // 8

Q_BLOCK = 128
KEY_CHUNK = 512
SMALL_W = 256
ROW_TILE = 512
FFN_ROW_TILE = 512
INT_MIN = -(2 ** 31)
RANK_LOWEST_FINITE = INT_MIN + (1 << 23)
NEG_BIG = -1e30


def _dot(a, b):
    return jnp.dot(a, b, preferred_element_type=F32)


def _dot_nt(a, b):
    return lax.dot_general(a, b, (((1,), (1,)), ((), ())), preferred_element_type=F32)


def _mod_kernel(c_ref, w_ref, b_ref, o_ref):
    c = c_ref[...]
    c_act = (c * jax.nn.sigmoid(c)).astype(BF16)
    o_ref[0] = _dot(c_act, w_ref[0].astype(BF16)) + b_ref[0]


def _modulation(c, w_ada, b_ada):
    depth, d, n = w_ada.shape
    b = c.shape[0]
    tn = n // 6
    return pl.pallas_call(
        _mod_kernel,
        grid=(depth, n // tn),
        in_specs=[
            pl.BlockSpec((b, d), lambda l, j: (0, 0)),
            pl.BlockSpec((1, d, tn), lambda l, j: (l, 0, j)),
            pl.BlockSpec((1, 1, tn), lambda l, j: (l, 0, j)),
        ],
        out_specs=pl.BlockSpec((1, b, tn), lambda l, j: (l, 0, j)),
        out_shape=jax.ShapeDtypeStruct((depth, b, n), F32),
        name="adaln_modulation",
    )(c, w_ada, b_ada.reshape(depth, 1, n))


def _group_mean_sq(v, gmat):
    return _dot((v * v).astype(BF16), gmat)


def _rope_cols(v, tab_c, tab_s1, tab_s2):
    cols = []
    for c0 in range(0, v.shape[1], LANES_V7X):
        vc = v[:, c0:c0 + LANES_V7X]
        fwd = pltpu.roll(vc, LANES_V7X - ROT_DIM // 2, axis=1)
        bwd = pltpu.roll(vc, ROT_DIM // 2, axis=1)
        cols.append(vc * tab_c + fwd * tab_s1 + bwd * tab_s2)
    return cols[0] if len(cols) == 1 else jnp.concatenate(cols, axis=1)


def _inproj_kernel(x_ref, mod_ref, g_ref, w_ref, b_ref, gmat_ref, tc_ref, ts1_ref, ts2_ref,
                   gain_ref, sgain_ref,
                   qa_ref, qi_ref, qb_ref, kb_ref, vbt_ref, ga_ref, gb_ref, ka_ref, ki_ref, vat_ref, wt_ref):
    d = x_ref.shape[1]
    x = x_ref[...]
    y = x * lax.rsqrt(jnp.mean(x * x, axis=-1, keepdims=True) + EPS) * g_ref[...]
    h = y * (1.0 + mod_ref[0, 1:2, :]) + mod_ref[0, 0:1, :]
    z = _dot(h.astype(BF16), w_ref[...]) + b_ref[...]

    gmat = gmat_ref[...]
    tab_c, tab_s1, tab_s2 = tc_ref[...], ts1_ref[...], ts2_ref[...]
    w = W_HEADS

    def head_norm(v, gain):
        return v * lax.rsqrt(_group_mean_sq(v, gmat) + EPS) * gain

    qa = head_norm(z[:, 0:w], gain_ref[0:1, :])
    qa_ref[...] = _rope_cols(qa, tab_c, tab_s1, tab_s2).astype(BF16)
    qi_ref[...] = _rope_cols(z[:, w:2 * w], tab_c, tab_s1, tab_s2).astype(BF16)
    qb = head_norm(z[:, 2 * w:3 * w], gain_ref[1:2, :]).astype(BF16)
    kb = head_norm(z[:, 3 * w:4 * w], gain_ref[2:3, :]).astype(BF16)
    for pr in range(HEAD_PAIRS):
        cols = slice(pr * LANES_V7X, (pr + 1) * LANES_V7X)
        qb_ref[pr] = qb[:, cols]
        kb_ref[pr] = kb[:, cols]
        for blk in range(vbt_ref.shape[1]):
            rows = slice(blk * BAND_Q, (blk + 1) * BAND_Q)
            vbt_ref[pr, blk] = z[rows, 4 * w + pr * LANES_V7X:4 * w + (pr + 1) * LANES_V7X].T.astype(BF16)
    ga_ref[...] = jax.nn.sigmoid(z[:, 5 * w:5 * w + d]).astype(BF16)
    gb_ref[...] = jax.nn.sigmoid(z[:, 5 * w + d:5 * w + 2 * d]).astype(BF16)

    sm = z[:, 5 * w + 2 * d:5 * w + 2 * d + SMALL_W]
    lane = lax.broadcasted_iota(I32, (1, LANES_V7X), 1)
    first_head = lane < HEAD_DIM
    c0 = sm[:, 0:LANES_V7X]
    c1 = sm[:, LANES_V7X:2 * LANES_V7X]
    ms0 = _group_mean_sq(c0, gmat[0:LANES_V7X, 0:LANES_V7X])
    c0 = jnp.where(first_head, c0 * lax.rsqrt(ms0 + EPS) * sgain_ref[0:1, :], c0)
    c1 = c1 * sgain_ref[1:2, :]
    t_c = jnp.where(first_head, tab_c, 1.0)
    t_s1 = jnp.where(first_head, tab_s1, 0.0)
    t_s2 = jnp.where(first_head, tab_s2, 0.0)
    c0 = _rope_cols(c0, t_c, t_s1, t_s2)
    c1 = _rope_cols(c1, t_c, t_s1, t_s2)
    ka_ref[...] = jnp.concatenate([c0[:, 0:HEAD_DIM]] * 2, axis=1).astype(BF16)
    ki_ref[...] = jnp.concatenate([c1[:, 0:HEAD_DIM]] * 2, axis=1).astype(BF16)
    vat_ref[...] = c0.T[HEAD_DIM:2 * HEAD_DIM, :].astype(BF16)
    w_t = c1.T[HEAD_DIM:HEAD_DIM + N_HEADS, :]
    for blk in range(wt_ref.shape[0]):
        wt_ref[blk] = w_t[:, blk * Q_BLOCK:(blk + 1) * Q_BLOCK]


def _const_spec(shape):
    nd = len(shape)
    return pl.BlockSpec(shape, lambda i: (0,) * nd, pipeline_mode=pl.Buffered(1))


def _in_projection(xf, mod_l, g_mix, w_big, b_big, gmat, tabs, gains, sgains, seq):
    t, d = xf.shape
    tm = ROW_TILE
    tiles_per_batch = seq // tm
    n_big = w_big.shape[1]
    row = lambda width: pl.BlockSpec((tm, width), lambda i: (i, 0))
    pair_major = pl.BlockSpec((HEAD_PAIRS, tm, LANES_V7X), lambda i: (0, i, 0))
    pm_shape = jax.ShapeDtypeStruct((HEAD_PAIRS, t, LANES_V7X), BF16)
    assert tm % BAND_Q == 0 and tm % Q_BLOCK == 0
    vbt_spec = pl.BlockSpec((HEAD_PAIRS, tm // BAND_Q, LANES_V7X, BAND_Q), lambda i: (0, i, 0, 0))
    vbt_shape = jax.ShapeDtypeStruct((HEAD_PAIRS, t // BAND_Q, LANES_V7X, BAND_Q), BF16)
    return pl.pallas_call(
        _inproj_kernel,
        grid=(t // tm,),
        in_specs=[
            row(d),
            pl.BlockSpec((1, 6, d), lambda i: (i // tiles_per_batch, 0, 0)),
            _const_spec((1, d)),
            _const_spec((d, n_big)),
            _const_spec((1, n_big)),
            _const_spec((W_HEADS, W_HEADS)),
            row(LANES_V7X), row(LANES_V7X), row(LANES_V7X),
            _const_spec((3, W_HEADS)),
            _const_spec((2, LANES_V7X)),
        ],
        out_specs=[row(W_HEADS), row(W_HEADS), pair_major, pair_major, vbt_spec, row(d), row(d),
                   row(LANES_V7X), row(LANES_V7X),
                   pl.BlockSpec((HEAD_DIM, tm), lambda i: (0, i)),
                   pl.BlockSpec((tm // Q_BLOCK, N_HEADS, Q_BLOCK), lambda i: (i, 0, 0))],
        out_shape=[jax.ShapeDtypeStruct((t, W_HEADS), BF16)] * 2 + [pm_shape] * 2 + [vbt_shape]
        + [jax.ShapeDtypeStruct((t, d), BF16)] * 2 + [jax.ShapeDtypeStruct((t, LANES_V7X), BF16)] * 2
        + [jax.ShapeDtypeStruct((HEAD_DIM, t), BF16), jax.ShapeDtypeStruct((t // Q_BLOCK, N_HEADS, Q_BLOCK), F32)],
        compiler_params=pltpu.CompilerParams(
            dimension_semantics=("arbitrary",), vmem_limit_bytes=VMEM_LIMIT),
        name="in_projection",
    )(xf, mod_l, g_mix, w_big, b_big, gmat, *tabs, gains, sgains)


def _sparse_kernel(qi_ref, qa_ref, wt_ref, ki_ref, ka_ref, vt_ref, o_ref,
                   qi_s, qa_s, score_s, s_scr, acc_s, tri_s, *, top_k):
    seq = qi_ref.shape[1]
    n_qb = seq // Q_BLOCK
    kc_rows = KEY_CHUNK
    sub_groups = kc_rows // SUBLANES_V7X
    wide = N_HEADS * Q_BLOCK
    r_i = lax.broadcasted_iota(I32, (kc_rows, kc_rows), 0)
    c_i = lax.broadcasted_iota(I32, (kc_rows, kc_rows), 1)
    tri_s[...] = jnp.where(c_i < r_i, 1.0, 0.0).astype(BF16)
    lane128 = lax.broadcasted_iota(I32, (1, LANES_V7X), 1)
    head_lanes = (lane128 < HEAD_DIM, lane128 >= HEAD_DIM)

    def float_of(key):
        return jnp.broadcast_to(
            lax.bitcast_convert_type(key ^ ((key >> 31) & 0x7FFFFFFF), F32), (kc_rows, Q_BLOCK))

    def attend(qb, n_chunks):
        r0 = pl.multiple_of(qb * Q_BLOCK, Q_BLOCK)
        chunks = [j * kc_rows for j in range(n_chunks)]
        w_idx = wt_ref[qb]
        lane = lax.broadcasted_iota(I32, (1, Q_BLOCK), 1)
        limit = (((r0 + lane) >> CHUNK_SHIFT) + 1) * CHUNK
        row_id = lax.broadcasted_iota(I32, (kc_rows, Q_BLOCK), 0)

        for k0 in chunks:
            logits = _dot_nt(ki_ref[0, k0:k0 + kc_rows, :], qi_s[...])
            score = jnp.zeros((kc_rows, Q_BLOCK), F32)
            for h in range(N_HEADS):
                score = score + jnp.maximum(logits[:, h * Q_BLOCK:(h + 1) * Q_BLOCK], 0.0) * w_idx[h:h + 1, :]
            if k0 + kc_rows > (n_chunks - 1) * kc_rows:
                score = jnp.where(row_id < limit - k0, score, -jnp.inf)
            score_s[k0:k0 + kc_rows, :] = score

        def count(pred_fn):
            acc = jnp.zeros((SUBLANES_V7X, Q_BLOCK), I32)
            for k0 in chunks:
                hit = jnp.where(pred_fn(score_s[k0:k0 + kc_rows, :]), 1, 0).astype(I32)
                acc = acc + hit.reshape(sub_groups, SUBLANES_V7X, Q_BLOCK).sum(axis=0)
            return acc.sum(axis=0, keepdims=True)

        def search_bit(i, state):
            thr, cnt_thr = state
            cand = thr + jnp.left_shift(jnp.int32(1), 31 - i)
            cand_f = float_of(cand)
            cnt = count(lambda sc: sc >= cand_f)
            take = cnt >= top_k
            return jnp.where(take, cand, thr), jnp.where(take, cnt, cnt_thr)

        thr0 = jnp.full((1, Q_BLOCK), INT_MIN, I32)
        cnt0 = jnp.full((1, Q_BLOCK), n_chunks * kc_rows, I32)
        need_search = r0 + Q_BLOCK > top_k
        thr, cnt_thr = lax.fori_loop(0, jnp.where(need_search, 32, 0), search_bit, (thr0, cnt0))
        cnt_thr = jnp.where(need_search, cnt_thr, top_k)
        thr = jnp.maximum(thr, RANK_LOWEST_FINITE)
        thr_f = float_of(thr)

        @pl.when(jnp.max(cnt_thr) > top_k)
        def _():
            next_f = float_of(thr + 1)
            need = (top_k - count(lambda sc: sc >= next_f)).astype(F32)
            seen = jnp.zeros((1, Q_BLOCK), F32)
            for k0 in chunks:
                sc = score_s[k0:k0 + kc_rows, :]
                eq = (sc >= thr_f) & (sc < next_f)
                eq_f = jnp.where(eq, 1.0, 0.0)
                rank = _dot(tri_s[...], eq_f.astype(BF16)) + seen
                score_s[k0:k0 + kc_rows, :] = jnp.where(eq & (rank >= need), -jnp.inf, sc)
                seen = seen + jnp.sum(eq_f, axis=0, keepdims=True)

        m_acc = jnp.full((SUBLANES_V7X, wide), -jnp.inf, F32)
        for k0 in chunks:
            s_all = _dot_nt(ka_ref[0, k0:k0 + kc_rows, :], qa_s[...])
            sel = score_s[k0:k0 + kc_rows, :] >= thr_f
            s = jnp.concatenate(
                [jnp.where(sel, s_all[:, h * Q_BLOCK:(h + 1) * Q_BLOCK], -jnp.inf) for h in range(N_HEADS)],
                axis=1)
            s_scr[k0:k0 + kc_rows, :] = s
            m_acc = jnp.maximum(m_acc, s.reshape(sub_groups, SUBLANES_V7X, wide).max(axis=0))
        m_fin = jnp.max(m_acc, axis=0, keepdims=True)

        l_acc = jnp.zeros((SUBLANES_V7X, wide), F32)
        o_t = jnp.zeros((HEAD_DIM, wide), F32)
        for k0 in chunks:
            p = jnp.exp2(s_scr[k0:k0 + kc_rows, :] - m_fin)
            o_t = o_t + _dot(vt_ref[:, k0:k0 + kc_rows], p.astype(BF16))
            l_acc = l_acc + p.reshape(sub_groups, SUBLANES_V7X, wide).sum(axis=0)
        acc_s[...] = o_t / jnp.sum(l_acc, axis=0, keepdims=True)

    def q_block(qb, carry):
        r0 = pl.multiple_of(qb * Q_BLOCK, Q_BLOCK)
        qi = qi_ref[0, pl.ds(r0, Q_BLOCK), :]
        qa = qa_ref[0, pl.ds(r0, Q_BLOCK), :]
        for h in range(N_HEADS):
            cols = slice((h // 2) * LANES_V7X, (h // 2 + 1) * LANES_V7X)
            qi_s[h * Q_BLOCK:(h + 1) * Q_BLOCK, :] = jnp.where(head_lanes[h % 2], qi[:, cols], jnp.zeros_like(qi[:, cols]))
            qa_s[h * Q_BLOCK:(h + 1) * Q_BLOCK, :] = jnp.where(head_lanes[h % 2], qa[:, cols], jnp.zeros_like(qa[:, cols]))
        n_chunks = (r0 + Q_BLOCK + kc_rows - 1) // kc_rows
        for n in range(1, seq // kc_rows + 1):
            pl.when(n_chunks == n)(functools.partial(attend, qb, n))
        o_t = acc_s[...]
        outs = [o_t[:, h * Q_BLOCK:(h + 1) * Q_BLOCK].T for h in range(N_HEADS)]
        o_ref[0, pl.ds(r0, Q_BLOCK), :] = jnp.concatenate(outs, axis=1).astype(BF16)
        return carry

    lax.fori_loop(0, n_qb, q_block, 0)


def _sparse_attention(qi, qa, w_t, ki, ka, v_t, top_k):
    b, seq, w = qi.shape
    batch3 = lambda s1, s2: pl.BlockSpec((1, s1, s2), lambda i: (i, 0, 0))
    return pl.pallas_call(
        functools.partial(_sparse_kernel, top_k=top_k),
        grid=(b,),
        in_specs=[
            batch3(seq, w), batch3(seq, w),
            pl.BlockSpec((seq // Q_BLOCK, N_HEADS, Q_BLOCK), lambda i: (i, 0, 0)),
            batch3(seq, LANES_V7X), batch3(seq, LANES_V7X),
            pl.BlockSpec((HEAD_DIM, seq), lambda i: (0, i)),
        ],
        out_specs=batch3(seq, w),
        out_shape=jax.ShapeDtypeStruct((b, seq, w), BF16),
        scratch_shapes=[
            pltpu.VMEM((N_HEADS * Q_BLOCK, LANES_V7X), BF16),
            pltpu.VMEM((N_HEADS * Q_BLOCK, LANES_V7X), BF16),
            pltpu.VMEM((seq, Q_BLOCK), F32),
            pltpu.VMEM((seq, N_HEADS * Q_BLOCK), F32),
            pltpu.VMEM((HEAD_DIM, N_HEADS * Q_BLOCK), F32),
            pltpu.VMEM((KEY_CHUNK, KEY_CHUNK), BF16),
        ],
        compiler_params=pltpu.CompilerParams(
            dimension_semantics=("arbitrary",), vmem_limit_bytes=VMEM_LIMIT),
        name="sparse_attention",
    )(qi, qa, w_t, ki, ka, v_t)


BAND_Q = 256
BAND_KEYS = N_PAST_CHUNKS * CHUNK + BAND_Q
BAND_PIECES = BAND_KEYS // BAND_Q
BIAS_ROWS_UNROLL = 8


def _band_kernel(q_ref, k_ref, vt_ref, bias_ref, o_ref, s_a, s_b):
    seq = q_ref.shape[2]
    n_items = (seq // BAND_Q) * HEAD_PAIRS
    groups = BAND_Q // SUBLANES_V7X
    lane = lax.broadcasted_iota(I32, (1, LANES_V7X), 1)
    head_lanes = (lane < HEAD_DIM, lane >= HEAD_DIM)

    def pieces_of(qb):
        out = []
        for w in range(BAND_PIECES):
            kb_idx = qb - (BAND_PIECES - 1) + w
            kb_c = jnp.maximum(kb_idx, 0)
            out.append((kb_c, pl.multiple_of(kb_c * BAND_Q, BAND_Q),
                        jnp.where(kb_idx >= 0, 0.0, NEG_BIG)))
        return out

    def logits(item, s_scr):
        qb, pr = item // HEAD_PAIRS, item % HEAD_PAIRS
        r0 = pl.multiple_of(qb * BAND_Q, BAND_Q)
        q2 = q_ref[pr, 0, pl.ds(r0, BAND_Q), :]
        for e in range(2):
            q_h = jnp.where(head_lanes[e], q2, jnp.zeros_like(q2))
            for w, (_, row, _) in enumerate(pieces_of(qb)):
                rows = slice(w * BAND_Q, (w + 1) * BAND_Q)
                k2 = k_ref[pr, 0, pl.ds(row, BAND_Q), :]
                s_scr[e, rows, :] = _dot_nt(k2, q_h) + bias_ref[2 * pr + e, rows, :]

    def softmax_values(item, s_scr):
        qb, pr = item // HEAD_PAIRS, item % HEAD_PAIRS
        r0 = pl.multiple_of(qb * BAND_Q, BAND_Q)
        pieces = pieces_of(qb)
        o_pair = []
        for e in range(2):
            m_acc = jnp.full((SUBLANES_V7X, BAND_Q), -jnp.inf, F32)
            for w, (_, _, pen) in enumerate(pieces):
                rows = slice(w * BAND_Q, (w + 1) * BAND_Q)
                m_acc = jnp.maximum(
                    m_acc, s_scr[e, rows, :].reshape(groups, SUBLANES_V7X, BAND_Q).max(axis=0) + pen)
            m_fin = jnp.max(m_acc, axis=0, keepdims=True)
            l_acc = jnp.zeros((SUBLANES_V7X, BAND_Q), F32)
            o_t = jnp.zeros((HEAD_DIM, BAND_Q), F32)
            for w, (kb_c, _, pen) in enumerate(pieces):
                rows = slice(w * BAND_Q, (w + 1) * BAND_Q)
                p = jnp.exp2(s_scr[e, rows, :] - (m_fin - pen))
                l_acc = l_acc + p.reshape(groups, SUBLANES_V7X, BAND_Q).sum(axis=0)
                v_h = vt_ref[pr, kb_c][e * HEAD_DIM:(e + 1) * HEAD_DIM, :]
                o_t = o_t + _dot(v_h, p.astype(BF16))
            o_pair.append(o_t / jnp.sum(l_acc, axis=0, keepdims=True))
        o_ref[pr, 0, pl.ds(r0, BAND_Q), :] = jnp.concatenate(o_pair, axis=0).T.astype(BF16)

    logits(0, s_a)

    def step(u, c):
        logits(2 * u + 1, s_b)
        softmax_values(2 * u, s_a)
        logits(2 * u + 2, s_a)
        softmax_values(2 * u + 1, s_b)
        return c

    lax.fori_loop(0, n_items // 2 - 1, step, 0)
    logits(n_items - 1, s_b)
    softmax_values(n_items - 2, s_a)
    softmax_values(n_items - 1, s_b)


def _band_attention(qb_pm, kb_pm, vb_t, bias_t):
    pairs, b, seq, lanes = qb_pm.shape
    batch4 = pl.BlockSpec((pairs, 1, seq, lanes), lambda i: (0, i, 0, 0))
    return pl.pallas_call(
        _band_kernel,
        grid=(b,),
        in_specs=[batch4, batch4,
                  pl.BlockSpec((pairs, seq // BAND_Q, lanes, BAND_Q), lambda i: (0, i, 0, 0)),
                  _const_spec(bias_t.shape)],
        out_specs=batch4,
        out_shape=jax.ShapeDtypeStruct(qb_pm.shape, BF16),
        scratch_shapes=[pltpu.VMEM((2, BAND_KEYS, BAND_Q), F32)] * 2,
        compiler_params=pltpu.CompilerParams(
            dimension_semantics=("arbitrary",), vmem_limit_bytes=VMEM_LIMIT),
        name="band_attention",
    )(qb_pm, kb_pm, vb_t, bias_t)


def _bias_table_kernel(vec_ref, o_ref):
    period = vec_ref.shape[2]
    base = pltpu.roll(jnp.broadcast_to(vec_ref[0], (SUBLANES_V7X, period)), 0, 1, stride=1, stride_axis=0)
    q_chunk = lax.broadcasted_iota(I32, (SUBLANES_V7X, BAND_Q), 1) >> CHUNK_SHIFT

    def rows(c, carry):
        for u in range(BIAS_ROWS_UNROLL):
            j0 = pl.multiple_of((c * BIAS_ROWS_UNROLL + u) * SUBLANES_V7X, SUBLANES_V7X)
            blk = pltpu.roll(base, j0, 1)[:, :BAND_Q]
            k_chunk = j0 >> CHUNK_SHIFT
            in_band = (q_chunk <= k_chunk) & (q_chunk + N_PAST_CHUNKS >= k_chunk)
            o_ref[0, pl.ds(j0, SUBLANES_V7X), :] = jnp.where(in_band, blk * LOG2_E, NEG_BIG)
        return carry

    lax.fori_loop(0, BAND_KEYS // (SUBLANES_V7X * BIAS_ROWS_UNROLL), rows, 0)


def _band_bias_t(rel_bias_l):
    n_rel = REL_HI - REL_LO + 1
    period = BAND_KEYS + BAND_Q
    pad = N_PAST_CHUNKS * CHUNK
    lo_pos = period - pad + REL_LO
    rb = rel_bias_l.astype(F32)
    hi_val, lo_val = rb[:, n_rel - 1:n_rel], rb[:, 0:1]
    vec = jnp.concatenate([
        jnp.broadcast_to(hi_val, (N_HEADS, BAND_Q + 1)),
        jnp.broadcast_to(lo_val, (N_HEADS, lo_pos - (BAND_Q + 1))),
        rb,
        jnp.broadcast_to(hi_val, (N_HEADS, period - lo_pos - n_rel)),
    ], axis=1)
    assert vec.shape[1] == period and period % LANES_V7X == 0
    return pl.pallas_call(
        _bias_table_kernel,
        grid=(N_HEADS,),
        in_specs=[pl.BlockSpec((1, 1, period), lambda h: (h, 0, 0))],
        out_specs=pl.BlockSpec((1, BAND_KEYS, BAND_Q), lambda h: (h, 0, 0)),
        out_shape=jax.ShapeDtypeStruct((N_HEADS, BAND_KEYS, BAND_Q), F32),
        name="band_bias_table",
    )(vec.reshape(N_HEADS, 1, period))


def _out_ffn_kernel(x_ref, a_ref, b_ref, ga_ref, gb_ref, mod_ref, woa_ref, wob_ref, wout_ref,
                    g_ref, wgu_ref, wdown_ref, o_ref):
    d_ff = wdown_ref.shape[0]
    y_a = _dot(a_ref[...], woa_ref[...])
    attn_b = jnp.concatenate([b_ref[pr] for pr in range(HEAD_PAIRS)], axis=1)
    y_b = _dot(attn_b, wob_ref[...])
    merged = ga_ref[...].astype(F32) * y_a + gb_ref[...].astype(F32) * y_b
    x1 = x_ref[...] + mod_ref[0, 2:3, :] * _dot(merged.astype(BF16), wout_ref[...])

    y = x1 * lax.rsqrt(jnp.mean(x1 * x1, axis=-1, keepdims=True) + EPS) * g_ref[...]
    h2 = y * (1.0 + mod_ref[0, 4:5, :]) + mod_ref[0, 3:4, :]
    gu = _dot(h2.astype(BF16), wgu_ref[...])
    gate, up = gu[:, 0:d_ff], gu[:, d_ff:2 * d_ff]
    act = gate * jax.nn.sigmoid(gate) * up
    o_ref[...] = x1 + mod_ref[0, 5:6, :] * _dot(act.astype(BF16), wdown_ref[...])


def _out_ffn(xf, attn_a, attn_b, ga, gb, mod_l, w_oa, w_ob, w_out, g_ffn, w_gu, w_down, seq):
    t, d = xf.shape
    tm = FFN_ROW_TILE
    tiles_per_batch = seq // tm
    row = lambda width: pl.BlockSpec((tm, width), lambda i: (i, 0))
    return pl.pallas_call(
        _out_ffn_kernel,
        grid=(t // tm,),
        in_specs=[
            row(d), row(W_HEADS), pl.BlockSpec((HEAD_PAIRS, tm, LANES_V7X), lambda i: (0, i, 0)), row(d), row(d),
            pl.BlockSpec((1, 6, d), lambda i: (i // tiles_per_batch, 0, 0)),
            _const_spec(w_oa.shape), _const_spec(w_ob.shape), _const_spec(w_out.shape),
            _const_spec((1, d)), _const_spec(w_gu.shape), _const_spec(w_down.shape),
        ],
        out_specs=row(d),
        out_shape=jax.ShapeDtypeStruct((t, d), F32),
        compiler_params=pltpu.CompilerParams(
            dimension_semantics=("arbitrary",), vmem_limit_bytes=VMEM_LIMIT_FFN),
        name="out_proj_ffn",
    )(xf, attn_a, attn_b, ga, gb, mod_l, w_oa, w_ob, w_out, g_ffn, w_gu, w_down)


def _rope_tables(positions):
    half = ROT_DIM // 2
    inv_freq = ROPE_THETA ** (-jnp.arange(0, ROT_DIM, 2, dtype=F32) / ROT_DIM)
    ang = positions.astype(F32).reshape(-1, 1) * inv_freq
    cos, sin = jnp.cos(ang), jnp.sin(ang)
    t = cos.shape[0]
    ones = jnp.ones((t, HEAD_DIM - ROT_DIM), F32)
    zeros = jnp.zeros((t, HEAD_DIM - half), F32)
    tab_c = jnp.concatenate([cos, cos, ones], axis=1)
    tab_s1 = jnp.concatenate([-sin, zeros], axis=1)
    tab_s2 = jnp.concatenate([jnp.zeros((t, half), F32), sin, zeros[:, half:]], axis=1)
    rep = LANES_V7X // HEAD_DIM
    return tuple(jnp.tile(tb, (1, rep)) for tb in (tab_c, tab_s1, tab_s2))


def _pack_in_projection(w_in_l, b_in_l, d):
    sizes = (W_HEADS, HEAD_DIM, HEAD_DIM, W_HEADS, HEAD_DIM, N_HEADS, W_HEADS, W_HEADS, W_HEADS, d, d)
    offs = [0]
    for s in sizes:
        offs.append(offs[-1] + s)
    seg = lambda a, i: a[..., offs[i]:offs[i + 1]]
    order = (0, 3, 6, 7, 8, 9, 10, 1, 2, 4, 5)
    pad = SMALL_W - (3 * HEAD_DIM + N_HEADS)
    w_bf = w_in_l.astype(BF16)
    w_big = jnp.concatenate([seg(w_bf, i) for i in order] + [jnp.zeros((d, pad), BF16)], axis=1)
    b_big = jnp.concatenate([seg(b_in_l, i) for i in order] + [jnp.zeros((pad,), b_in_l.dtype)], axis=0)
    return w_big, b_big.reshape(1, -1).astype(F32)


def kernel(x, c, positions, w_ada, b_ada, g_mix, w_in, b_in, qn_a, kn_a, qn_b, kn_b, rel_bias,
           w_oa, w_ob, w_out, g_ffn, w_gu, w_down):
    b, seq, d = x.shape
    depth = w_ada.shape[0]
    t = b * seq
    top_k = min(TOPK_MAX, seq // 4)
    assert seq % KEY_CHUNK == 0 and seq % ROW_TILE == 0 and seq % FFN_ROW_TILE == 0 and seq % BAND_Q == 0 and d % LANES_V7X == 0

    mod = _modulation(c, w_ada, b_ada).reshape(depth, b, 6, d)
    tabs = _rope_tables(positions)
    group = jnp.arange(W_HEADS) // HEAD_DIM
    gmat = jnp.where(group[:, None] == group[None, :], 1.0 / HEAD_DIM, 0.0).astype(BF16)
    lane = jnp.arange(LANES_V7X)
    tile_heads = lambda g: jnp.tile(g.astype(F32), N_HEADS)

    xf = x.reshape(t, d)
    for l in range(depth):
        w_big, b_big = _pack_in_projection(w_in[l], b_in[l], d)
        q_scale = ATTN_SCALE * LOG2_E
        gains = jnp.stack([tile_heads(qn_a[l]) * q_scale, tile_heads(qn_b[l]) * q_scale, tile_heads(kn_b[l])])
        sgains = jnp.stack([
            jnp.concatenate([kn_a[l].astype(F32), jnp.ones((LANES_V7X - HEAD_DIM,), F32)]),
            jnp.where((lane >= HEAD_DIM) & (lane < HEAD_DIM + N_HEADS), INDEX_SCALE, 1.0).astype(F32)])
        qa, qi, qb, kb, vb_t, ga, gb, ka, ki, va_t, w_t = _in_projection(
            xf, mod[l], g_mix[l].reshape(1, d), w_big, b_big, gmat, tabs, gains, sgains, seq)

        to3 = lambda a: a.reshape(b, seq, a.shape[-1])
        attn_a = _sparse_attention(to3(qi), to3(qa), w_t, to3(ki), to3(ka), va_t, top_k)
        pm4 = lambda a: a.reshape(HEAD_PAIRS, b, seq, LANES_V7X)
        attn_b = _band_attention(pm4(qb), pm4(kb), vb_t, _band_bias_t(rel_bias[l]))

        xf = _out_ffn(xf, attn_a.reshape(t, W_HEADS), attn_b.reshape(HEAD_PAIRS, t, LANES_V7X), ga, gb, mod[l],
                      w_oa[l].astype(BF16), w_ob[l].astype(BF16), w_out[l].astype(BF16),
                      g_ffn[l].reshape(1, d), w_gu[l].astype(BF16), w_down[l].astype(BF16), seq)
    return xf.reshape(b, seq, d)
```

```python
import functools

import jax
import jax.numpy as jnp
from jax import lax
from jax.experimental import pallas as pl
from jax.experimental.pallas import tpu as pltpu

F32 = jnp.float32
BF16 = jnp.bfloat16
I32 = jnp.int32

CHUNK = 64
CHUNK_SHIFT = CHUNK.bit_length() - 1
HEAD_DIM = 64
N_HEADS = 8
TOPK_MAX = 256
N_PAST_CHUNKS = 8
REL_LO = -(CHUNK - 1)
REL_HI = 256
ROT_DIM = HEAD_DIM // 4
ROPE_THETA = 500000.0
EPS = 1e-6
W_HEADS = N_HEADS * HEAD_DIM
INDEX_SCALE = (HEAD_DIM ** -0.5) * (N_HEADS ** -0.5)
ATTN_SCALE = HEAD_DIM ** -0.5
LOG2_E = 1.4426950408889634
HEAD_PAIRS = N_HEADS // 2

LANES_V7X = 128
SUBLANES_V7X = 8
VMEM_BYTES_V7X = 64 * 1024 * 1024
VMEM_LIMIT = VMEM_BYTES_V7X * 3 // 4
VMEM_LIMIT_FFN = VMEM_BYTES_V7X * 7 // 8

Q_BLOCK = 128
KEY_CHUNK = 512
SMALL_W = 256
ROW_TILE = 512
FFN_ROW_TILE = 512
INT_MIN = -(2 ** 31)
RANK_LOWEST_FINITE = INT_MIN + (1 << 23)
NEG_BIG = -1e30


def _dot(a, b):
    return jnp.dot(a, b, preferred_element_type=F32)


def _dot_nt(a, b):
    return lax.dot_general(a, b, (((1,), (1,)), ((), ())), preferred_element_type=F32)


def _mod_kernel(c_ref, w_ref, b_ref, o_ref):
    c = c_ref[...]
    c_act = (c * jax.nn.sigmoid(c)).astype(BF16)
    o_ref[0] = _dot(c_act, w_ref[0].astype(BF16)) + b_ref[0]


def _modulation(c, w_ada, b_ada):
    depth, d, n = w_ada.shape
    b = c.shape[0]
    tn = n // 6
    return pl.pallas_call(
        _mod_kernel,
        grid=(depth, n // tn),
        in_specs=[
            pl.BlockSpec((b, d), lambda l, j: (0, 0)),
            pl.BlockSpec((1, d, tn), lambda l, j: (l, 0, j)),
            pl.BlockSpec((1, 1, tn), lambda l, j: (l, 0, j)),
        ],
        out_specs=pl.BlockSpec((1, b, tn), lambda l, j: (l, 0, j)),
        out_shape=jax.ShapeDtypeStruct((depth, b, n), F32),
        name="adaln_modulation",
    )(c, w_ada, b_ada.reshape(depth, 1, n))


def _group_mean_sq(v, gmat):
    return _dot((v * v).astype(BF16), gmat)


def _rope_cols(v, tab_c, tab_s1, tab_s2):
    cols = []
    for c0 in range(0, v.shape[1], LANES_V7X):
        vc = v[:, c0:c0 + LANES_V7X]
        fwd = pltpu.roll(vc, LANES_V7X - ROT_DIM // 2, axis=1)
        bwd = pltpu.roll(vc, ROT_DIM // 2, axis=1)
        cols.append(vc * tab_c + fwd * tab_s1 + bwd * tab_s2)
    return cols[0] if len(cols) == 1 else jnp.concatenate(cols, axis=1)


def _inproj_kernel(x_ref, mod_ref, g_ref, w_ref, b_ref, gmat_ref, tc_ref, ts1_ref, ts2_ref,
                   gain_ref, sgain_ref,
                   qa_ref, qi_ref, qb_ref, kb_ref, vbt_ref, ga_ref, gb_ref, ka_ref, ki_ref, vat_ref, wt_ref):
    d = x_ref.shape[1]
    x = x_ref[...]
    y = x * lax.rsqrt(jnp.mean(x * x, axis=-1, keepdims=True) + EPS) * g_ref[...]
    h = y * (1.0 + mod_ref[0, 1:2, :]) + mod_ref[0, 0:1, :]
    z = _dot(h.astype(BF16), w_ref[...]) + b_ref[...]

    gmat = gmat_ref[...]
    tab_c, tab_s1, tab_s2 = tc_ref[...], ts1_ref[...], ts2_ref[...]
    w = W_HEADS

    def head_norm(v, gain):
        return v * lax.rsqrt(_group_mean_sq(v, gmat) + EPS) * gain

    qa = head_norm(z[:, 0:w], gain_ref[0:1, :])
    qa_ref[...] = _rope_cols(qa, tab_c, tab_s1, tab_s2).astype(BF16)
    qi_ref[...] = _rope_cols(z[:, w:2 * w], tab_c, tab_s1, tab_s2).astype(BF16)
    qb = head_norm(z[:, 2 * w:3 * w], gain_ref[1:2, :]).astype(BF16)
    kb = head_norm(z[:, 3 * w:4 * w], gain_ref[2:3, :]).astype(BF16)
    for pr in range(HEAD_PAIRS):
        cols = slice(pr * LANES_V7X, (pr + 1) * LANES_V7X)
        qb_ref[pr] = qb[:, cols]
        kb_ref[pr] = kb[:, cols]
        for blk in range(vbt_ref.shape[1]):
            rows = slice(blk * BAND_Q, (blk + 1) * BAND_Q)
            vbt_ref[pr, blk] = z[rows, 4 * w + pr * LANES_V7X:4 * w + (pr + 1) * LANES_V7X].T.astype(BF16)
    ga_ref[...] = jax.nn.sigmoid(z[:, 5 * w:5 * w + d]).astype(BF16)
    gb_ref[...] = jax.nn.sigmoid(z[:, 5 * w + d:5 * w + 2 * d]).astype(BF16)

    sm = z[:, 5 * w + 2 * d:5 * w + 2 * d + SMALL_W]
    lane = lax.broadcasted_iota(I32, (1, LANES_V7X), 1)
    first_head = lane < HEAD_DIM
    c0 = sm[:, 0:LANES_V7X]
    c1 = sm[:, LANES_V7X:2 * LANES_V7X]
    ms0 = _group_mean_sq(c0, gmat[0:LANES_V7X, 0:LANES_V7X])
    c0 = jnp.where(first_head, c0 * lax.rsqrt(ms0 + EPS) * sgain_ref[0:1, :], c0)
    c1 = c1 * sgain_ref[1:2, :]
    t_c = jnp.where(first_head, tab_c, 1.0)
    t_s1 = jnp.where(first_head, tab_s1, 0.0)
    t_s2 = jnp.where(first_head, tab_s2, 0.0)
    c0 = _rope_cols(c0, t_c, t_s1, t_s2)
    c1 = _rope_cols(c1, t_c, t_s1, t_s2)
    ka_ref[...] = jnp.concatenate([c0[:, 0:HEAD_DIM]] * 2, axis=1).astype(BF16)
    ki_ref[...] = jnp.concatenate([c1[:, 0:HEAD_DIM]] * 2, axis=1).astype(BF16)
    vat_ref[...] = c0.T[HEAD_DIM:2 * HEAD_DIM, :].astype(BF16)
    w_t = c1.T[HEAD_DIM:HEAD_DIM + N_HEADS, :]
    for blk in range(wt_ref.shape[0]):
        wt_ref[blk] = w_t[:, blk * Q_BLOCK:(blk + 1) * Q_BLOCK]


def _const_spec(shape):
    nd = len(shape)
    return pl.BlockSpec(shape, lambda i: (0,) * nd, pipeline_mode=pl.Buffered(1))


def _in_projection(xf, mod_l, g_mix, w_big, b_big, gmat, tabs, gains, sgains, seq):
    t, d = xf.shape
    tm = ROW_TILE
    tiles_per_batch = seq // tm
    n_big = w_big.shape[1]
    row = lambda width: pl.BlockSpec((tm, width), lambda i: (i, 0))
    pair_major = pl.BlockSpec((HEAD_PAIRS, tm, LANES_V7X), lambda i: (0, i, 0))
    pm_shape = jax.ShapeDtypeStruct((HEAD_PAIRS, t, LANES_V7X), BF16)
    assert tm % BAND_Q == 0 and tm % Q_BLOCK == 0
    vbt_spec = pl.BlockSpec((HEAD_PAIRS, tm // BAND_Q, LANES_V7X, BAND_Q), lambda i: (0, i, 0, 0))
    vbt_shape = jax.ShapeDtypeStruct((HEAD_PAIRS, t // BAND_Q, LANES_V7X, BAND_Q), BF16)
    return pl.pallas_call(
        _inproj_kernel,
        grid=(t // tm,),
        in_specs=[
            row(d),
            pl.BlockSpec((1, 6, d), lambda i: (i // tiles_per_batch, 0, 0)),
            _const_spec((1, d)),
            _const_spec((d, n_big)),
            _const_spec((1, n_big)),
            _const_spec((W_HEADS, W_HEADS)),
            row(LANES_V7X), row(LANES_V7X), row(LANES_V7X),
            _const_spec((3, W_HEADS)),
            _const_spec((2, LANES_V7X)),
        ],
        out_specs=[row(W_HEADS), row(W_HEADS), pair_major, pair_major, vbt_spec, row(d), row(d),
                   row(LANES_V7X), row(LANES_V7X),
                   pl.BlockSpec((HEAD_DIM, tm), lambda i: (0, i)),
                   pl.BlockSpec((tm // Q_BLOCK, N_HEADS, Q_BLOCK), lambda i: (i, 0, 0))],
        out_shape=[jax.ShapeDtypeStruct((t, W_HEADS), BF16)] * 2 + [pm_shape] * 2 + [vbt_shape]
        + [jax.ShapeDtypeStruct((t, d), BF16)] * 2 + [jax.ShapeDtypeStruct((t, LANES_V7X), BF16)] * 2
        + [jax.ShapeDtypeStruct((HEAD_DIM, t), BF16), jax.ShapeDtypeStruct((t // Q_BLOCK, N_HEADS, Q_BLOCK), F32)],
        compiler_params=pltpu.CompilerParams(
            dimension_semantics=("arbitrary",), vmem_limit_bytes=VMEM_LIMIT),
        name="in_projection",
    )(xf, mod_l, g_mix, w_big, b_big, gmat, *tabs, gains, sgains)


def _sparse_kernel(qi_ref, qa_ref, wt_ref, ki_ref, ka_ref, vt_ref, o_ref,
                   qi_s, qa_s, score_s, s_scr, acc_s, tri_s, *, top_k):
    seq = qi_ref.shape[1]
    n_qb = seq // Q_BLOCK
    kc_rows = KEY_CHUNK
    sub_groups = kc_rows // SUBLANES_V7X
    wide = N_HEADS * Q_BLOCK
    r_i = lax.broadcasted_iota(I32, (kc_rows, kc_rows), 0)
    c_i = lax.broadcasted_iota(I32, (kc_rows, kc_rows), 1)
    tri_s[...] = jnp.where(c_i < r_i, 1.0, 0.0).astype(BF16)
    lane128 = lax.broadcasted_iota(I32, (1, LANES_V7X), 1)
    head_lanes = (lane128 < HEAD_DIM, lane128 >= HEAD_DIM)

    def float_of(key):
        return jnp.broadcast_to(
            lax.bitcast_convert_type(key ^ ((key >> 31) & 0x7FFFFFFF), F32), (kc_rows, Q_BLOCK))

    def attend(qb, n_chunks):
        r0 = pl.multiple_of(qb * Q_BLOCK, Q_BLOCK)
        chunks = [j * kc_rows for j in range(n_chunks)]
        w_idx = wt_ref[qb]
        lane = lax.broadcasted_iota(I32, (1, Q_BLOCK), 1)
        limit = (((r0 + lane) >> CHUNK_SHIFT) + 1) * CHUNK
        row_id = lax.broadcasted_iota(I32, (kc_rows, Q_BLOCK), 0)

        for k0 in chunks:
            logits = _dot_nt(ki_ref[0, k0:k0 + kc_rows, :], qi_s[...])
            score = jnp.zeros((kc_rows, Q_BLOCK), F32)
            for h in range(N_HEADS):
                score = score + jnp.maximum(logits[:, h * Q_BLOCK:(h + 1) * Q_BLOCK], 0.0) * w_idx[h:h + 1, :]
            if k0 + kc_rows > (n_chunks - 1) * kc_rows:
                score = jnp.where(row_id < limit - k0, score, -jnp.inf)
            score_s[k0:k0 + kc_rows, :] = score

        def count(pred_fn):
            acc = jnp.zeros((SUBLANES_V7X, Q_BLOCK), I32)
            for k0 in chunks:
                hit = jnp.where(pred_fn(score_s[k0:k0 + kc_rows, :]), 1, 0).astype(I32)
                acc = acc + hit.reshape(sub_groups, SUBLANES_V7X, Q_BLOCK).sum(axis=0)
            return acc.sum(axis=0, keepdims=True)

        def search_bit(i, state):
            thr, cnt_thr = state
            cand = thr + jnp.left_shift(jnp.int32(1), 31 - i)
            cand_f = float_of(cand)
            cnt = count(lambda sc: sc >= cand_f)
            take = cnt >= top_k
            return jnp.where(take, cand, thr), jnp.where(take, cnt, cnt_thr)

        thr0 = jnp.full((1, Q_BLOCK), INT_MIN, I32)
        cnt0 = jnp.full((1, Q_BLOCK), n_chunks * kc_rows, I32)
        need_search = r0 + Q_BLOCK > top_k
        thr, cnt_thr = lax.fori_loop(0, jnp.where(need_search, 32, 0), search_bit, (thr0, cnt0))
        cnt_thr = jnp.where(need_search, cnt_thr, top_k)
        thr = jnp.maximum(thr, RANK_LOWEST_FINITE)
        thr_f = float_of(thr)

        @pl.when(jnp.max(cnt_thr) > top_k)
        def _():
            next_f = float_of(thr + 1)
            need = (top_k - count(lambda sc: sc >= next_f)).astype(F32)
            seen = jnp.zeros((1, Q_BLOCK), F32)
            for k0 in chunks:
                sc = score_s[k0:k0 + kc_rows, :]
                eq = (sc >= thr_f) & (sc < next_f)
                eq_f = jnp.where(eq, 1.0, 0.0)
                rank = _dot(tri_s[...], eq_f.astype(BF16)) + seen
                score_s[k0:k0 + kc_rows, :] = jnp.where(eq & (rank >= need), -jnp.inf, sc)
                seen = seen + jnp.sum(eq_f, axis=0, keepdims=True)

        m_acc = jnp.full((SUBLANES_V7X, wide), -jnp.inf, F32)
        for k0 in chunks:
            s_all = _dot_nt(ka_ref[0, k0:k0 + kc_rows, :], qa_s[...])
            sel = score_s[k0:k0 + kc_rows, :] >= thr_f
            s = jnp.concatenate(
                [jnp.where(sel, s_all[:, h * Q_BLOCK:(h + 1) * Q_BLOCK], -jnp.inf) for h in range(N_HEADS)],
                axis=1)
            s_scr[k0:k0 + kc_rows, :] = s
            m_acc = jnp.maximum(m_acc, s.reshape(sub_groups, SUBLANES_V7X, wide).max(axis=0))
        m_fin = jnp.max(m_acc, axis=0, keepdims=True)

        l_acc = jnp.zeros((SUBLANES_V7X, wide), F32)
        o_t = jnp.zeros((HEAD_DIM, wide), F32)
        for k0 in chunks:
            p = jnp.exp2(s_scr[k0:k0 + kc_rows, :] - m_fin)
            o_t = o_t + _dot(vt_ref[:, k0:k0 + kc_rows], p.astype(BF16))
            l_acc = l_acc + p.reshape(sub_groups, SUBLANES_V7X, wide).sum(axis=0)
        acc_s[...] = o_t / jnp.sum(l_acc, axis=0, keepdims=True)

    def q_block(qb, carry):
        r0 = pl.multiple_of(qb * Q_BLOCK, Q_BLOCK)
        qi = qi_ref[0, pl.ds(r0, Q_BLOCK), :]
        qa = qa_ref[0, pl.ds(r0, Q_BLOCK), :]
        for h in range(N_HEADS):
            cols = slice((h // 2) * LANES_V7X, (h // 2 + 1) * LANES_V7X)
            qi_s[h * Q_BLOCK:(h + 1) * Q_BLOCK, :] = jnp.where(head_lanes[h % 2], qi[:, cols], jnp.zeros_like(qi[:, cols]))
            qa_s[h * Q_BLOCK:(h + 1) * Q_BLOCK, :] = jnp.where(head_lanes[h % 2], qa[:, cols], jnp.zeros_like(qa[:, cols]))
        n_chunks = (r0 + Q_BLOCK + kc_rows - 1) // kc_rows
        for n in range(1, seq // kc_rows + 1):
            pl.when(n_chunks == n)(functools.partial(attend, qb, n))
        o_t = acc_s[...]
        outs = [o_t[:, h * Q_BLOCK:(h + 1) * Q_BLOCK].T for h in range(N_HEADS)]
        o_ref[0, pl.ds(r0, Q_BLOCK), :] = jnp.concatenate(outs, axis=1).astype(BF16)
        return carry

    lax.fori_loop(0, n_qb, q_block, 0)


def _sparse_attention(qi, qa, w_t, ki, ka, v_t, top_k):
    b, seq, w = qi.shape
    batch3 = lambda s1, s2: pl.BlockSpec((1, s1, s2), lambda i: (i, 0, 0))
    return pl.pallas_call(
        functools.partial(_sparse_kernel, top_k=top_k),
        grid=(b,),
        in_specs=[
            batch3(seq, w), batch3(seq, w),
            pl.BlockSpec((seq // Q_BLOCK, N_HEADS, Q_BLOCK), lambda i: (i, 0, 0)),
            batch3(seq, LANES_V7X), batch3(seq, LANES_V7X),
            pl.BlockSpec((HEAD_DIM, seq), lambda i: (0, i)),
        ],
        out_specs=batch3(seq, w),
        out_shape=jax.ShapeDtypeStruct((b, seq, w), BF16),
        scratch_shapes=[
            pltpu.VMEM((N_HEADS * Q_BLOCK, LANES_V7X), BF16),
            pltpu.VMEM((N_HEADS * Q_BLOCK, LANES_V7X), BF16),
            pltpu.VMEM((seq, Q_BLOCK), F32),
            pltpu.VMEM((seq, N_HEADS * Q_BLOCK), F32),
            pltpu.VMEM((HEAD_DIM, N_HEADS * Q_BLOCK), F32),
            pltpu.VMEM((KEY_CHUNK, KEY_CHUNK), BF16),
        ],
        compiler_params=pltpu.CompilerParams(
            dimension_semantics=("arbitrary",), vmem_limit_bytes=VMEM_LIMIT),
        name="sparse_attention",
    )(qi, qa, w_t, ki, ka, v_t)


BAND_Q = 256
BAND_KEYS = N_PAST_CHUNKS * CHUNK + BAND_Q
BAND_PIECES = BAND_KEYS // BAND_Q
BIAS_ROWS_UNROLL = 8


def _band_kernel(q_ref, k_ref, vt_ref, bias_ref, o_ref, s_a, s_b):
    seq = q_ref.shape[2]
    groups = BAND_Q // SUBLANES_V7X
    lane = lax.broadcasted_iota(I32, (1, LANES_V7X), 1)
    head_lanes = (lane < HEAD_DIM, lane >= HEAD_DIM)
    items = [(qb, pr) for qb in range(seq // BAND_Q) for pr in range(HEAD_PAIRS)]

    def pieces_of(qb):
        first = qb - (BAND_PIECES - 1)
        return [(w, first + w) for w in range(BAND_PIECES) if first + w >= 0]

    def logits(item, s_scr):
        qb, pr = item
        q2 = q_ref[pr, 0, qb * BAND_Q:(qb + 1) * BAND_Q, :]
        for e in range(2):
            q_h = jnp.where(head_lanes[e], q2, jnp.zeros_like(q2))
            for w, kb in pieces_of(qb):
                rows = slice(w * BAND_Q, (w + 1) * BAND_Q)
                k2 = k_ref[pr, 0, kb * BAND_Q:(kb + 1) * BAND_Q, :]
                s_scr[e, rows, :] = _dot_nt(k2, q_h) + bias_ref[2 * pr + e, rows, :]

    def softmax_values(item, s_scr):
        qb, pr = item
        pieces = pieces_of(qb)
        o_pair = []
        for e in range(2):
            m_acc = jnp.full((SUBLANES_V7X, BAND_Q), -jnp.inf, F32)
            for w, _ in pieces:
                rows = slice(w * BAND_Q, (w + 1) * BAND_Q)
                m_acc = jnp.maximum(m_acc, s_scr[e, rows, :].reshape(groups, SUBLANES_V7X, BAND_Q).max(axis=0))
            m_fin = jnp.max(m_acc, axis=0, keepdims=True)
            l_acc = jnp.zeros((SUBLANES_V7X, BAND_Q), F32)
            o_t = jnp.zeros((HEAD_DIM, BAND_Q), F32)
            for w, kb in pieces:
                rows = slice(w * BAND_Q, (w + 1) * BAND_Q)
                p = jnp.exp2(s_scr[e, rows, :] - m_fin)
                l_acc = l_acc + p.reshape(groups, SUBLANES_V7X, BAND_Q).sum(axis=0)
                v_h = vt_ref[pr, kb][e * HEAD_DIM:(e + 1) * HEAD_DIM, :]
                o_t = o_t + _dot(v_h, p.astype(BF16))
            o_pair.append(o_t / jnp.sum(l_acc, axis=0, keepdims=True))
        o_ref[pr, 0, qb * BAND_Q:(qb + 1) * BAND_Q, :] = jnp.concatenate(o_pair, axis=0).T.astype(BF16)

    bufs = (s_a, s_b)
    logits(items[0], bufs[0])
    for t in range(1, len(items)):
        logits(items[t], bufs[t % 2])
        softmax_values(items[t - 1], bufs[(t - 1) % 2])
    softmax_values(items[-1], bufs[(len(items) - 1) % 2])


def _band_attention(qb_pm, kb_pm, vb_t, bias_t):
    pairs, b, seq, lanes = qb_pm.shape
    batch4 = pl.BlockSpec((pairs, 1, seq, lanes), lambda i: (0, i, 0, 0))
    return pl.pallas_call(
        _band_kernel,
        grid=(b,),
        in_specs=[batch4, batch4,
                  pl.BlockSpec((pairs, seq // BAND_Q, lanes, BAND_Q), lambda i: (0, i, 0, 0)),
                  _const_spec(bias_t.shape)],
        out_specs=batch4,
        out_shape=jax.ShapeDtypeStruct(qb_pm.shape, BF16),
        scratch_shapes=[pltpu.VMEM((2, BAND_KEYS, BAND_Q), F32)] * 2,
        compiler_params=pltpu.CompilerParams(
            dimension_semantics=("arbitrary",), vmem_limit_bytes=VMEM_LIMIT),
        name="band_attention",
    )(qb_pm, kb_pm, vb_t, bias_t)


def _bias_table_kernel(vec_ref, o_ref):
    period = vec_ref.shape[2]
    base = pltpu.roll(jnp.broadcast_to(vec_ref[0], (SUBLANES_V7X, period)), 0, 1, stride=1, stride_axis=0)
    q_chunk = lax.broadcasted_iota(I32, (SUBLANES_V7X, BAND_Q), 1) >> CHUNK_SHIFT

    def rows(c, carry):
        for u in range(BIAS_ROWS_UNROLL):
            j0 = pl.multiple_of((c * BIAS_ROWS_UNROLL + u) * SUBLANES_V7X, SUBLANES_V7X)
            blk = pltpu.roll(base, j0, 1)[:, :BAND_Q]
            k_chunk = j0 >> CHUNK_SHIFT
            in_band = (q_chunk <= k_chunk) & (q_chunk + N_PAST_CHUNKS >= k_chunk)
            o_ref[0, pl.ds(j0, SUBLANES_V7X), :] = jnp.where(in_band, blk * LOG2_E, NEG_BIG)
        return carry

    lax.fori_loop(0, BAND_KEYS // (SUBLANES_V7X * BIAS_ROWS_UNROLL), rows, 0)


def _band_bias_t(rel_bias_l):
    n_rel = REL_HI - REL_LO + 1
    period = BAND_KEYS + BAND_Q
    pad = N_PAST_CHUNKS * CHUNK
    lo_pos = period - pad + REL_LO
    rb = rel_bias_l.astype(F32)
    hi_val, lo_val = rb[:, n_rel - 1:n_rel], rb[:, 0:1]
    vec = jnp.concatenate([
        jnp.broadcast_to(hi_val, (N_HEADS, BAND_Q + 1)),
        jnp.broadcast_to(lo_val, (N_HEADS, lo_pos - (BAND_Q + 1))),
        rb,
        jnp.broadcast_to(hi_val, (N_HEADS, period - lo_pos - n_rel)),
    ], axis=1)
    assert vec.shape[1] == period and period % LANES_V7X == 0
    return pl.pallas_call(
        _bias_table_kernel,
        grid=(N_HEADS,),
        in_specs=[pl.BlockSpec((1, 1, period), lambda h: (h, 0, 0))],
        out_specs=pl.BlockSpec((1, BAND_KEYS, BAND_Q), lambda h: (h, 0, 0)),
        out_shape=jax.ShapeDtypeStruct((N_HEADS, BAND_KEYS, BAND_Q), F32),
        name="band_bias_table",
    )(vec.reshape(N_HEADS, 1, period))


def _out_ffn_kernel(x_ref, a_ref, b_ref, ga_ref, gb_ref, mod_ref, woa_ref, wob_ref, wout_ref,
                    g_ref, wgu_ref, wdown_ref, o_ref):
    d_ff = wdown_ref.shape[0]
    y_a = _dot(a_ref[...], woa_ref[...])
    attn_b = jnp.concatenate([b_ref[pr] for pr in range(HEAD_PAIRS)], axis=1)
    y_b = _dot(attn_b, wob_ref[...])
    merged = ga_ref[...].astype(F32) * y_a + gb_ref[...].astype(F32) * y_b
    x1 = x_ref[...] + mod_ref[0, 2:3, :] * _dot(merged.astype(BF16), wout_ref[...])

    y = x1 * lax.rsqrt(jnp.mean(x1 * x1, axis=-1, keepdims=True) + EPS) * g_ref[...]
    h2 = y * (1.0 + mod_ref[0, 4:5, :]) + mod_ref[0, 3:4, :]
    gu = _dot(h2.astype(BF16), wgu_ref[...])
    gate, up = gu[:, 0:d_ff], gu[:, d_ff:2 * d_ff]
    act = gate * jax.nn.sigmoid(gate) * up
    o_ref[...] = x1 + mod_ref[0, 5:6, :] * _dot(act.astype(BF16), wdown_ref[...])


def _out_ffn(xf, attn_a, attn_b, ga, gb, mod_l, w_oa, w_ob, w_out, g_ffn, w_gu, w_down, seq):
    t, d = xf.shape
    tm = FFN_ROW_TILE
    tiles_per_batch = seq // tm
    row = lambda width: pl.BlockSpec((tm, width), lambda i: (i, 0))
    return pl.pallas_call(
        _out_ffn_kernel,
        grid=(t // tm,),
        in_specs=[
            row(d), row(W_HEADS), pl.BlockSpec((HEAD_PAIRS, tm, LANES_V7X), lambda i: (0, i, 0)), row(d), row(d),
            pl.BlockSpec((1, 6, d), lambda i: (i // tiles_per_batch, 0, 0)),
            _const_spec(w_oa.shape), _const_spec(w_ob.shape), _const_spec(w_out.shape),
            _const_spec((1, d)), _const_spec(w_gu.shape), _const_spec(w_down.shape),
        ],
        out_specs=row(d),
        out_shape=jax.ShapeDtypeStruct((t, d), F32),
        compiler_params=pltpu.CompilerParams(
            dimension_semantics=("arbitrary",), vmem_limit_bytes=VMEM_LIMIT_FFN),
        name="out_proj_ffn",
    )(xf, attn_a, attn_b, ga, gb, mod_l, w_oa, w_ob, w_out, g_ffn, w_gu, w_down)


def _rope_tables(positions):
    half = ROT_DIM // 2
    inv_freq = ROPE_THETA ** (-jnp.arange(0, ROT_DIM, 2, dtype=F32) / ROT_DIM)
    ang = positions.astype(F32).reshape(-1, 1) * inv_freq
    cos, sin = jnp.cos(ang), jnp.sin(ang)
    t = cos.shape[0]
    ones = jnp.ones((t, HEAD_DIM - ROT_DIM), F32)
    zeros = jnp.zeros((t, HEAD_DIM - half), F32)
    tab_c = jnp.concatenate([cos, cos, ones], axis=1)
    tab_s1 = jnp.concatenate([-sin, zeros], axis=1)
    tab_s2 = jnp.concatenate([jnp.zeros((t, half), F32), sin, zeros[:, half:]], axis=1)
    rep = LANES_V7X // HEAD_DIM
    return tuple(jnp.tile(tb, (1, rep)) for tb in (tab_c, tab_s1, tab_s2))


def _pack_in_projection(w_in_l, b_in_l, d):
    sizes = (W_HEADS, HEAD_DIM, HEAD_DIM, W_HEADS, HEAD_DIM, N_HEADS, W_HEADS, W_HEADS, W_HEADS, d, d)
    offs = [0]
    for s in sizes:
        offs.append(offs[-1] + s)
    seg = lambda a, i: a[..., offs[i]:offs[i + 1]]
    order = (0, 3, 6, 7, 8, 9, 10, 1, 2, 4, 5)
    pad = SMALL_W - (3 * HEAD_DIM + N_HEADS)
    w_bf = w_in_l.astype(BF16)
    w_big = jnp.concatenate([seg(w_bf, i) for i in order] + [jnp.zeros((d, pad), BF16)], axis=1)
    b_big = jnp.concatenate([seg(b_in_l, i) for i in order] + [jnp.zeros((pad,), b_in_l.dtype)], axis=0)
    return w_big, b_big.reshape(1, -1).astype(F32)


def kernel(x, c, positions, w_ada, b_ada, g_mix, w_in, b_in, qn_a, kn_a, qn_b, kn_b, rel_bias,
           w_oa, w_ob, w_out, g_ffn, w_gu, w_down):
    b, seq, d = x.shape
    depth = w_ada.shape[0]
    t = b * seq
    top_k = min(TOPK_MAX, seq // 4)
    assert seq % KEY_CHUNK == 0 and seq % ROW_TILE == 0 and seq % FFN_ROW_TILE == 0 and seq % BAND_Q == 0 and d % LANES_V7X == 0

    mod = _modulation(c, w_ada, b_ada).reshape(depth, b, 6, d)
    tabs = _rope_tables(positions)
    group = jnp.arange(W_HEADS) // HEAD_DIM
    gmat = jnp.where(group[:, None] == group[None, :], 1.0 / HEAD_DIM, 0.0).astype(BF16)
    lane = jnp.arange(LANES_V7X)
    tile_heads = lambda g: jnp.tile(g.astype(F32), N_HEADS)

    xf = x.reshape(t, d)
    for l in range(depth):
        w_big, b_big = _pack_in_projection(w_in[l], b_in[l], d)
        q_scale = ATTN_SCALE * LOG2_E
        gains = jnp.stack([tile_heads(qn_a[l]) * q_scale, tile_heads(qn_b[l]) * q_scale, tile_heads(kn_b[l])])
        sgains = jnp.stack([
            jnp.concatenate([kn_a[l].astype(F32), jnp.ones((LANES_V7X - HEAD_DIM,), F32)]),
            jnp.where((lane >= HEAD_DIM) & (lane < HEAD_DIM + N_HEADS), INDEX_SCALE, 1.0).astype(F32)])
        qa, qi, qb, kb, vb_t, ga, gb, ka, ki, va_t, w_t = _in_projection(
            xf, mod[l], g_mix[l].reshape(1, d), w_big, b_big, gmat, tabs, gains, sgains, seq)

        to3 = lambda a: a.reshape(b, seq, a.shape[-1])
        attn_a = _sparse_attention(to3(qi), to3(qa), w_t, to3(ki), to3(ka), va_t, top_k)
        pm4 = lambda a: a.reshape(HEAD_PAIRS, b, seq, LANES_V7X)
        attn_b = _band_attention(pm4(qb), pm4(kb), vb_t, _band_bias_t(rel_bias[l]))

        xf = _out_ffn(xf, attn_a.reshape(t, W_HEADS), attn_b.reshape(HEAD_PAIRS, t, LANES_V7X), ga, gb, mod[l],
                      w_oa[l].astype(BF16), w_ob[l].astype(BF16), w_out[l].astype(BF16),
                      g_ffn[l].reshape(1, d), w_gu[l].astype(BF16), w_down[l].astype(BF16), seq)
    return xf.reshape(b, seq, d)
```

```python
import functools

import jax
import jax.numpy as jnp
from jax import lax
from jax.experimental import pallas as pl
from jax.experimental.pallas import tpu as pltpu

F32 = jnp.float32
BF16 = jnp.bfloat16
I32 = jnp.int32

CHUNK = 64
CHUNK_SHIFT = CHUNK.bit_length() - 1
HEAD_DIM = 64
N_HEADS = 8
TOPK_MAX = 256
N_PAST_CHUNKS = 8
REL_LO = -(CHUNK - 1)
REL_HI = 256
ROT_DIM = HEAD_DIM // 4
ROPE_THETA = 500000.0
EPS = 1e-6
W_HEADS = N_HEADS * HEAD_DIM
INDEX_SCALE = (HEAD_DIM ** -0.5) * (N_HEADS ** -0.5)
ATTN_SCALE = HEAD_DIM ** -0.5
LOG2_E = 1.4426950408889634
HEAD_PAIRS = N_HEADS // 2

LANES_V7X = 128
SUBLANES_V7X = 8
VMEM_BYTES_V7X = 64 * 1024 * 1024
VMEM_LIMIT = VMEM_BYTES_V7X * 3 // 4
VMEM_LIMIT_FFN = VMEM_BYTES_V7X * 7 // 8

Q_BLOCK = 128
KEY_CHUNK = 512
SMALL_W = 256
ROW_TILE = 512
FFN_ROW_TILE = 512
INT_MIN = -(2 ** 31)
RANK_LOWEST_FINITE = INT_MIN + (1 << 23)
NEG_BIG = -1e30


def _dot(a, b):
    return jnp.dot(a, b, preferred_element_type=F32)


def _dot_nt(a, b):
    return lax.dot_general(a, b, (((1,), (1,)), ((), ())), preferred_element_type=F32)


def _mod_kernel(c_ref, w_ref, b_ref, o_ref):
    c = c_ref[...]
    c_act = (c * jax.nn.sigmoid(c)).astype(BF16)
    o_ref[0] = _dot(c_act, w_ref[0].astype(BF16)) + b_ref[0]


def _modulation(c, w_ada, b_ada):
    depth, d, n = w_ada.shape
    b = c.shape[0]
    tn = n // 6
    return pl.pallas_call(
        _mod_kernel,
        grid=(depth, n // tn),
        in_specs=[
            pl.BlockSpec((b, d), lambda l, j: (0, 0)),
            pl.BlockSpec((1, d, tn), lambda l, j: (l, 0, j)),
            pl.BlockSpec((1, 1, tn), lambda l, j: (l, 0, j)),
        ],
        out_specs=pl.BlockSpec((1, b, tn), lambda l, j: (l, 0, j)),
        out_shape=jax.ShapeDtypeStruct((depth, b, n), F32),
        name="adaln_modulation",
    )(c, w_ada, b_ada.reshape(depth, 1, n))


def _group_mean_sq(v, gmat):
    return _dot((v * v).astype(BF16), gmat)


def _rope_cols(v, tab_c, tab_s1, tab_s2):
    cols = []
    for c0 in range(0, v.shape[1], LANES_V7X):
        vc = v[:, c0:c0 + LANES_V7X]
        fwd = pltpu.roll(vc, LANES_V7X - ROT_DIM // 2, axis=1)
        bwd = pltpu.roll(vc, ROT_DIM // 2, axis=1)
        cols.append(vc * tab_c + fwd * tab_s1 + bwd * tab_s2)
    return cols[0] if len(cols) == 1 else jnp.concatenate(cols, axis=1)


def _inproj_kernel(x_ref, mod_ref, g_ref, w_ref, b_ref, gmat_ref, tab_ref,
                   gain_ref, sgain_ref,
                   qa_ref, qi_ref, qb_ref, kb_ref, vbt_ref, ga_ref, gb_ref, ka_ref, ki_ref, vat_ref, wt_ref):
    d = x_ref.shape[1]
    x = x_ref[...]
    y = x * lax.rsqrt(jnp.mean(x * x, axis=-1, keepdims=True) + EPS) * g_ref[...]
    h = y * (1.0 + mod_ref[0, 1:2, :]) + mod_ref[0, 0:1, :]
    z = _dot(h.astype(BF16), w_ref[...]) + b_ref[...]

    gmat = gmat_ref[...]
    lane = lax.broadcasted_iota(I32, (1, LANES_V7X), 1)
    first_head = lane < HEAD_DIM
    t0, t1 = tab_ref[:, 0:LANES_V7X], tab_ref[:, LANES_V7X:2 * LANES_V7X]
    t0_swapped = pltpu.roll(t0, HEAD_DIM, axis=1)
    tab_c = jnp.where(first_head, t0, t0_swapped)
    tab_s1 = jnp.where(first_head, t0_swapped, t0)
    tab_s2 = jnp.where(first_head, t1, pltpu.roll(t1, HEAD_DIM, axis=1))
    w = W_HEADS

    def head_norm(v, gain):
        return v * lax.rsqrt(_group_mean_sq(v, gmat) + EPS) * gain

    qa = head_norm(z[:, 0:w], gain_ref[0:1, :])
    qa_ref[...] = _rope_cols(qa, tab_c, tab_s1, tab_s2).astype(BF16)
    qi_ref[...] = _rope_cols(z[:, w:2 * w], tab_c, tab_s1, tab_s2).astype(BF16)
    qb = head_norm(z[:, 2 * w:3 * w], gain_ref[1:2, :]).astype(BF16)
    kb = head_norm(z[:, 3 * w:4 * w], gain_ref[2:3, :]).astype(BF16)
    for pr in range(HEAD_PAIRS):
        cols = slice(pr * LANES_V7X, (pr + 1) * LANES_V7X)
        qb_ref[pr] = qb[:, cols]
        kb_ref[pr] = kb[:, cols]
        for blk in range(vbt_ref.shape[1]):
            rows = slice(blk * BAND_Q, (blk + 1) * BAND_Q)
            vbt_ref[pr, blk] = z[rows, 4 * w + pr * LANES_V7X:4 * w + (pr + 1) * LANES_V7X].T.astype(BF16)
    ga_ref[...] = jax.nn.sigmoid(z[:, 5 * w:5 * w + d]).astype(BF16)
    gb_ref[...] = jax.nn.sigmoid(z[:, 5 * w + d:5 * w + 2 * d]).astype(BF16)

    sm = z[:, 5 * w + 2 * d:5 * w + 2 * d + SMALL_W]
    c0 = sm[:, 0:LANES_V7X]
    c1 = sm[:, LANES_V7X:2 * LANES_V7X]
    ms0 = _group_mean_sq(c0, gmat[0:LANES_V7X, 0:LANES_V7X])
    c0 = jnp.where(first_head, c0 * lax.rsqrt(ms0 + EPS) * sgain_ref[0:1, :], c0)
    c1 = c1 * sgain_ref[1:2, :]
    t_c = jnp.where(first_head, tab_c, 1.0)
    t_s1 = jnp.where(first_head, tab_s1, 0.0)
    t_s2 = jnp.where(first_head, tab_s2, 0.0)
    c0 = _rope_cols(c0, t_c, t_s1, t_s2)
    c1 = _rope_cols(c1, t_c, t_s1, t_s2)
    ka_ref[...] = jnp.concatenate([c0[:, 0:HEAD_DIM]] * 2, axis=1).astype(BF16)
    ki_ref[...] = jnp.concatenate([c1[:, 0:HEAD_DIM]] * 2, axis=1).astype(BF16)
    vat_ref[...] = c0.T[HEAD_DIM:2 * HEAD_DIM, :].astype(BF16)
    w_t = c1.T[HEAD_DIM:HEAD_DIM + N_HEADS, :]
    for blk in range(wt_ref.shape[0]):
        wt_ref[blk] = w_t[:, blk * Q_BLOCK:(blk + 1) * Q_BLOCK]


def _const_spec(shape):
    nd = len(shape)
    return pl.BlockSpec(shape, lambda i: (0,) * nd, pipeline_mode=pl.Buffered(1))


def _in_projection(xf, mod_l, g_mix, w_big, b_big, gmat, tabs, gains, sgains, seq):
    t, d = xf.shape
    tm = ROW_TILE
    tiles_per_batch = seq // tm
    n_big = w_big.shape[1]
    row = lambda width: pl.BlockSpec((tm, width), lambda i: (i, 0))
    pair_major = pl.BlockSpec((HEAD_PAIRS, tm, LANES_V7X), lambda i: (0, i, 0))
    pm_shape = jax.ShapeDtypeStruct((HEAD_PAIRS, t, LANES_V7X), BF16)
    assert tm % BAND_Q == 0 and tm % Q_BLOCK == 0
    vbt_spec = pl.BlockSpec((HEAD_PAIRS, tm // BAND_Q, LANES_V7X, BAND_Q), lambda i: (0, i, 0, 0))
    vbt_shape = jax.ShapeDtypeStruct((HEAD_PAIRS, t // BAND_Q, LANES_V7X, BAND_Q), BF16)
    return pl.pallas_call(
        _inproj_kernel,
        grid=(t // tm,),
        in_specs=[
            row(d),
            pl.BlockSpec((1, 6, d), lambda i: (i // tiles_per_batch, 0, 0)),
            _const_spec((1, d)),
            _const_spec((d, n_big)),
            _const_spec((1, n_big)),
            _const_spec((W_HEADS, W_HEADS)),
            row(2 * LANES_V7X),
            _const_spec((3, W_HEADS)),
            _const_spec((2, LANES_V7X)),
        ],
        out_specs=[row(W_HEADS), row(W_HEADS), pair_major, pair_major, vbt_spec, row(d), row(d),
                   row(LANES_V7X), row(LANES_V7X),
                   pl.BlockSpec((HEAD_DIM, tm), lambda i: (0, i)),
                   pl.BlockSpec((tm // Q_BLOCK, N_HEADS, Q_BLOCK), lambda i: (i, 0, 0))],
        out_shape=[jax.ShapeDtypeStruct((t, W_HEADS), BF16)] * 2 + [pm_shape] * 2 + [vbt_shape]
        + [jax.ShapeDtypeStruct((t, d), BF16)] * 2 + [jax.ShapeDtypeStruct((t, LANES_V7X), BF16)] * 2
        + [jax.ShapeDtypeStruct((HEAD_DIM, t), BF16), jax.ShapeDtypeStruct((t // Q_BLOCK, N_HEADS, Q_BLOCK), F32)],
        compiler_params=pltpu.CompilerParams(
            dimension_semantics=("arbitrary",), vmem_limit_bytes=VMEM_LIMIT),
        name="in_projection",
    )(xf, mod_l, g_mix, w_big, b_big, gmat, tabs, gains, sgains)


def _sparse_kernel(qi_ref, qa_ref, wt_ref, ki_ref, ka_ref, vt_ref, o_ref,
                   qi_s, qa_s, score_s, s_scr, acc_s, tri_s, *, top_k):
    seq = qi_ref.shape[1]
    n_qb = seq // Q_BLOCK
    kc_rows = KEY_CHUNK
    sub_groups = kc_rows // SUBLANES_V7X
    wide = N_HEADS * Q_BLOCK
    r_i = lax.broadcasted_iota(I32, (kc_rows, kc_rows), 0)
    c_i = lax.broadcasted_iota(I32, (kc_rows, kc_rows), 1)
    tri_s[...] = jnp.where(c_i < r_i, 1.0, 0.0).astype(BF16)
    lane128 = lax.broadcasted_iota(I32, (1, LANES_V7X), 1)
    head_lanes = (lane128 < HEAD_DIM, lane128 >= HEAD_DIM)

    def float_of(key):
        return jnp.broadcast_to(
            lax.bitcast_convert_type(key ^ ((key >> 31) & 0x7FFFFFFF), F32), (kc_rows, Q_BLOCK))

    def attend(qb, n_chunks):
        r0 = pl.multiple_of(qb * Q_BLOCK, Q_BLOCK)
        chunks = [j * kc_rows for j in range(n_chunks)]
        w_idx = wt_ref[qb]
        lane = lax.broadcasted_iota(I32, (1, Q_BLOCK), 1)
        limit = (((r0 + lane) >> CHUNK_SHIFT) + 1) * CHUNK
        row_id = lax.broadcasted_iota(I32, (kc_rows, Q_BLOCK), 0)

        for k0 in chunks:
            logits = _dot_nt(ki_ref[0, k0:k0 + kc_rows, :], qi_s[...])
            score = jnp.zeros((kc_rows, Q_BLOCK), F32)
            for h in range(N_HEADS):
                score = score + jnp.maximum(logits[:, h * Q_BLOCK:(h + 1) * Q_BLOCK], 0.0) * w_idx[h:h + 1, :]
            if k0 + kc_rows > (n_chunks - 1) * kc_rows:
                score = jnp.where(row_id < limit - k0, score, -jnp.inf)
            score_s[k0:k0 + kc_rows, :] = score

        def count(pred_fn):
            acc = jnp.zeros((SUBLANES_V7X, Q_BLOCK), I32)
            for k0 in chunks:
                hit = jnp.where(pred_fn(score_s[k0:k0 + kc_rows, :]), 1, 0).astype(I32)
                acc = acc + hit.reshape(sub_groups, SUBLANES_V7X, Q_BLOCK).sum(axis=0)
            return acc.sum(axis=0, keepdims=True)

        def search_bit(i, state):
            thr, cnt_thr = state
            cand = thr + jnp.left_shift(jnp.int32(1), 31 - i)
            cand_f = float_of(cand)
            cnt = count(lambda sc: sc >= cand_f)
            take = cnt >= top_k
            return jnp.where(take, cand, thr), jnp.where(take, cnt, cnt_thr)

        thr0 = jnp.full((1, Q_BLOCK), INT_MIN, I32)
        cnt0 = jnp.full((1, Q_BLOCK), n_chunks * kc_rows, I32)
        need_search = r0 + Q_BLOCK > top_k
        thr, cnt_thr = lax.fori_loop(0, jnp.where(need_search, 32, 0), search_bit, (thr0, cnt0))
        cnt_thr = jnp.where(need_search, cnt_thr, top_k)
        thr = jnp.maximum(thr, RANK_LOWEST_FINITE)
        thr_f = float_of(thr)

        @pl.when(jnp.max(cnt_thr) > top_k)
        def _():
            next_f = float_of(thr + 1)
            need = (top_k - count(lambda sc: sc >= next_f)).astype(F32)
            seen = jnp.zeros((1, Q_BLOCK), F32)
            for k0 in chunks:
                sc = score_s[k0:k0 + kc_rows, :]
                eq = (sc >= thr_f) & (sc < next_f)
                eq_f = jnp.where(eq, 1.0, 0.0)
                rank = _dot(tri_s[...], eq_f.astype(BF16)) + seen
                score_s[k0:k0 + kc_rows, :] = jnp.where(eq & (rank >= need), -jnp.inf, sc)
                seen = seen + jnp.sum(eq_f, axis=0, keepdims=True)

        m_acc = jnp.full((SUBLANES_V7X, wide), -jnp.inf, F32)
        for k0 in chunks:
            s_all = _dot_nt(ka_ref[0, k0:k0 + kc_rows, :], qa_s[...])
            sel = score_s[k0:k0 + kc_rows, :] >= thr_f
            s = jnp.concatenate(
                [jnp.where(sel, s_all[:, h * Q_BLOCK:(h + 1) * Q_BLOCK], -jnp.inf) for h in range(N_HEADS)],
                axis=1)
            s_scr[k0:k0 + kc_rows, :] = s
            m_acc = jnp.maximum(m_acc, s.reshape(sub_groups, SUBLANES_V7X, wide).max(axis=0))
        m_fin = jnp.max(m_acc, axis=0, keepdims=True)

        l_acc = jnp.zeros((SUBLANES_V7X, wide), F32)
        o_t = jnp.zeros((HEAD_DIM, wide), F32)
        for k0 in chunks:
            p = jnp.exp2(s_scr[k0:k0 + kc_rows, :] - m_fin)
            o_t = o_t + _dot(vt_ref[:, k0:k0 + kc_rows], p.astype(BF16))
            l_acc = l_acc + p.reshape(sub_groups, SUBLANES_V7X, wide).sum(axis=0)
        acc_s[...] = o_t / jnp.sum(l_acc, axis=0, keepdims=True)

    def q_block(qb, carry):
        r0 = pl.multiple_of(qb * Q_BLOCK, Q_BLOCK)
        qi = qi_ref[0, pl.ds(r0, Q_BLOCK), :]
        qa = qa_ref[0, pl.ds(r0, Q_BLOCK), :]
        for h in range(N_HEADS):
            cols = slice((h // 2) * LANES_V7X, (h // 2 + 1) * LANES_V7X)
            qi_s[h * Q_BLOCK:(h + 1) * Q_BLOCK, :] = jnp.where(head_lanes[h % 2], qi[:, cols], jnp.zeros_like(qi[:, cols]))
            qa_s[h * Q_BLOCK:(h + 1) * Q_BLOCK, :] = jnp.where(head_lanes[h % 2], qa[:, cols], jnp.zeros_like(qa[:, cols]))
        n_chunks = (r0 + Q_BLOCK + kc_rows - 1) // kc_rows
        for n in range(1, seq // kc_rows + 1):
            pl.when(n_chunks == n)(functools.partial(attend, qb, n))
        o_t = acc_s[...]
        outs = [o_t[:, h * Q_BLOCK:(h + 1) * Q_BLOCK].T for h in range(N_HEADS)]
        o_ref[0, pl.ds(r0, Q_BLOCK), :] = jnp.concatenate(outs, axis=1).astype(BF16)
        return carry

    lax.fori_loop(0, n_qb, q_block, 0)


def _sparse_attention(qi, qa, w_t, ki, ka, v_t, top_k):
    b, seq, w = qi.shape
    batch3 = lambda s1, s2: pl.BlockSpec((1, s1, s2), lambda i: (i, 0, 0))
    return pl.pallas_call(
        functools.partial(_sparse_kernel, top_k=top_k),
        grid=(b,),
        in_specs=[
            batch3(seq, w), batch3(seq, w),
            pl.BlockSpec((seq // Q_BLOCK, N_HEADS, Q_BLOCK), lambda i: (i, 0, 0)),
            batch3(seq, LANES_V7X), batch3(seq, LANES_V7X),
            pl.BlockSpec((HEAD_DIM, seq), lambda i: (0, i)),
        ],
        out_specs=batch3(seq, w),
        out_shape=jax.ShapeDtypeStruct((b, seq, w), BF16),
        scratch_shapes=[
            pltpu.VMEM((N_HEADS * Q_BLOCK, LANES_V7X), BF16),
            pltpu.VMEM((N_HEADS * Q_BLOCK, LANES_V7X), BF16),
            pltpu.VMEM((seq, Q_BLOCK), F32),
            pltpu.VMEM((seq, N_HEADS * Q_BLOCK), F32),
            pltpu.VMEM((HEAD_DIM, N_HEADS * Q_BLOCK), F32),
            pltpu.VMEM((KEY_CHUNK, KEY_CHUNK), BF16),
        ],
        compiler_params=pltpu.CompilerParams(
            dimension_semantics=("arbitrary",), vmem_limit_bytes=VMEM_LIMIT),
        name="sparse_attention",
    )(qi, qa, w_t, ki, ka, v_t)


BAND_Q = 256
BAND_KEYS = N_PAST_CHUNKS * CHUNK + BAND_Q
BAND_PIECES = BAND_KEYS // BAND_Q
BIAS_ROWS_UNROLL = 8


def _band_kernel(q_ref, k_ref, vt_ref, bias_ref, o_ref, s_a, s_b):
    seq = q_ref.shape[2]
    groups = BAND_Q // SUBLANES_V7X
    lane = lax.broadcasted_iota(I32, (1, LANES_V7X), 1)
    head_lanes = (lane < HEAD_DIM, lane >= HEAD_DIM)
    items = [(qb, pr) for qb in range(seq // BAND_Q) for pr in range(HEAD_PAIRS)]

    def pieces_of(qb):
        first = qb - (BAND_PIECES - 1)
        return [(w, first + w) for w in range(BAND_PIECES) if first + w >= 0]

    def logits(item, s_scr):
        qb, pr = item
        q2 = q_ref[pr, 0, qb * BAND_Q:(qb + 1) * BAND_Q, :]
        for e in range(2):
            q_h = jnp.where(head_lanes[e], q2, jnp.zeros_like(q2))
            for w, kb in pieces_of(qb):
                rows = slice(w * BAND_Q, (w + 1) * BAND_Q)
                k2 = k_ref[pr, 0, kb * BAND_Q:(kb + 1) * BAND_Q, :]
                s_scr[e, rows, :] = _dot_nt(k2, q_h) + bias_ref[2 * pr + e, rows, :]

    def softmax_values(item, s_scr):
        qb, pr = item
        pieces = pieces_of(qb)
        o_pair = []
        for e in range(2):
            m_acc = jnp.full((SUBLANES_V7X, BAND_Q), -jnp.inf, F32)
            for w, _ in pieces:
                rows = slice(w * BAND_Q, (w + 1) * BAND_Q)
                m_acc = jnp.maximum(m_acc, s_scr[e, rows, :].reshape(groups, SUBLANES_V7X, BAND_Q).max(axis=0))
            m_fin = jnp.max(m_acc, axis=0, keepdims=True)
            l_acc = jnp.zeros((SUBLANES_V7X, BAND_Q), F32)
            o_t = jnp.zeros((HEAD_DIM, BAND_Q), F32)
            for w, kb in pieces:
                rows = slice(w * BAND_Q, (w + 1) * BAND_Q)
                p = jnp.exp2(s_scr[e, rows, :] - m_fin)
                l_acc = l_acc + p.reshape(groups, SUBLANES_V7X, BAND_Q).sum(axis=0)
                v_h = vt_ref[pr, kb][e * HEAD_DIM:(e + 1) * HEAD_DIM, :]
                o_t = o_t + _dot(v_h, p.astype(BF16))
            o_pair.append(o_t / jnp.sum(l_acc, axis=0, keepdims=True))
        o_ref[pr, 0, qb * BAND_Q:(qb + 1) * BAND_Q, :] = jnp.concatenate(o_pair, axis=0).T.astype(BF16)

    bufs = (s_a, s_b)
    logits(items[0], bufs[0])
    for t in range(1, len(items)):
        logits(items[t], bufs[t % 2])
        softmax_values(items[t - 1], bufs[(t - 1) % 2])
    softmax_values(items[-1], bufs[(len(items) - 1) % 2])


def _band_attention(qb_pm, kb_pm, vb_t, bias_t):
    pairs, b, seq, lanes = qb_pm.shape
    batch4 = pl.BlockSpec((pairs, 1, seq, lanes), lambda i: (0, i, 0, 0))
    return pl.pallas_call(
        _band_kernel,
        grid=(b,),
        in_specs=[batch4, batch4,
                  pl.BlockSpec((pairs, seq // BAND_Q, lanes, BAND_Q), lambda i: (0, i, 0, 0)),
                  _const_spec(bias_t.shape)],
        out_specs=batch4,
        out_shape=jax.ShapeDtypeStruct(qb_pm.shape, BF16),
        scratch_shapes=[pltpu.VMEM((2, BAND_KEYS, BAND_Q), F32)] * 2,
        compiler_params=pltpu.CompilerParams(
            dimension_semantics=("arbitrary",), vmem_limit_bytes=VMEM_LIMIT),
        name="band_attention",
    )(qb_pm, kb_pm, vb_t, bias_t)


def _bias_table_kernel(vec_ref, o_ref):
    period = vec_ref.shape[2]
    base = pltpu.roll(jnp.broadcast_to(vec_ref[0], (SUBLANES_V7X, period)), 0, 1, stride=1, stride_axis=0)
    q_chunk = lax.broadcasted_iota(I32, (SUBLANES_V7X, BAND_Q), 1) >> CHUNK_SHIFT

    def rows(c, carry):
        for u in range(BIAS_ROWS_UNROLL):
            j0 = pl.multiple_of((c * BIAS_ROWS_UNROLL + u) * SUBLANES_V7X, SUBLANES_V7X)
            blk = pltpu.roll(base, j0, 1)[:, :BAND_Q]
            k_chunk = j0 >> CHUNK_SHIFT
            in_band = (q_chunk <= k_chunk) & (q_chunk + N_PAST_CHUNKS >= k_chunk)
            o_ref[0, pl.ds(j0, SUBLANES_V7X), :] = jnp.where(in_band, blk * LOG2_E, NEG_BIG)
        return carry

    lax.fori_loop(0, BAND_KEYS // (SUBLANES_V7X * BIAS_ROWS_UNROLL), rows, 0)


def _band_bias_t(rel_bias_l):
    n_rel = REL_HI - REL_LO + 1
    period = BAND_KEYS + BAND_Q
    pad = N_PAST_CHUNKS * CHUNK
    lo_pos = period - pad + REL_LO
    rb = rel_bias_l.astype(F32)
    hi_val, lo_val = rb[:, n_rel - 1:n_rel], rb[:, 0:1]
    vec = jnp.concatenate([
        jnp.broadcast_to(hi_val, (N_HEADS, BAND_Q + 1)),
        jnp.broadcast_to(lo_val, (N_HEADS, lo_pos - (BAND_Q + 1))),
        rb,
        jnp.broadcast_to(hi_val, (N_HEADS, period - lo_pos - n_rel)),
    ], axis=1)
    assert vec.shape[1] == period and period % LANES_V7X == 0
    return pl.pallas_call(
        _bias_table_kernel,
        grid=(N_HEADS,),
        in_specs=[pl.BlockSpec((1, 1, period), lambda h: (h, 0, 0))],
        out_specs=pl.BlockSpec((1, BAND_KEYS, BAND_Q), lambda h: (h, 0, 0)),
        out_shape=jax.ShapeDtypeStruct((N_HEADS, BAND_KEYS, BAND_Q), F32),
        name="band_bias_table",
    )(vec.reshape(N_HEADS, 1, period))


def _out_ffn_kernel(x_ref, a_ref, b_ref, ga_ref, gb_ref, mod_ref, woa_ref, wob_ref, wout_ref,
                    g_ref, wgu_ref, wdown_ref, o_ref):
    d_ff = wdown_ref.shape[0]
    y_a = _dot(a_ref[...], woa_ref[...])
    attn_b = jnp.concatenate([b_ref[pr] for pr in range(HEAD_PAIRS)], axis=1)
    y_b = _dot(attn_b, wob_ref[...])
    merged = ga_ref[...].astype(F32) * y_a + gb_ref[...].astype(F32) * y_b
    x1 = x_ref[...] + mod_ref[0, 2:3, :] * _dot(merged.astype(BF16), wout_ref[...])

    y = x1 * lax.rsqrt(jnp.mean(x1 * x1, axis=-1, keepdims=True) + EPS) * g_ref[...]
    h2 = y * (1.0 + mod_ref[0, 4:5, :]) + mod_ref[0, 3:4, :]
    gu = _dot(h2.astype(BF16), wgu_ref[...])
    gate, up = gu[:, 0:d_ff], gu[:, d_ff:2 * d_ff]
    act = gate * jax.nn.sigmoid(gate) * up
    o_ref[...] = x1 + mod_ref[0, 5:6, :] * _dot(act.astype(BF16), wdown_ref[...])


def _out_ffn(xf, attn_a, attn_b, ga, gb, mod_l, w_oa, w_ob, w_out, g_ffn, w_gu, w_down, seq):
    t, d = xf.shape
    tm = FFN_ROW_TILE
    tiles_per_batch = seq // tm
    row = lambda width: pl.BlockSpec((tm, width), lambda i: (i, 0))
    return pl.pallas_call(
        _out_ffn_kernel,
        grid=(t // tm,),
        in_specs=[
            row(d), row(W_HEADS), pl.BlockSpec((HEAD_PAIRS, tm, LANES_V7X), lambda i: (0, i, 0)), row(d), row(d),
            pl.BlockSpec((1, 6, d), lambda i: (i // tiles_per_batch, 0, 0)),
            _const_spec(w_oa.shape), _const_spec(w_ob.shape), _const_spec(w_out.shape),
            _const_spec((1, d)), _const_spec(w_gu.shape), _const_spec(w_down.shape),
        ],
        out_specs=row(d),
        out_shape=jax.ShapeDtypeStruct((t, d), F32),
        compiler_params=pltpu.CompilerParams(
            dimension_semantics=("arbitrary",), vmem_limit_bytes=VMEM_LIMIT_FFN),
        name="out_proj_ffn",
    )(xf, attn_a, attn_b, ga, gb, mod_l, w_oa, w_ob, w_out, g_ffn, w_gu, w_down)


def _rope_tables(positions):
    half = ROT_DIM // 2
    inv_freq = ROPE_THETA ** (-jnp.arange(0, ROT_DIM, 2, dtype=F32) / ROT_DIM)
    ang = positions.astype(F32).reshape(-1, 1) * inv_freq
    cos, sin = jnp.cos(ang), jnp.sin(ang)
    t = cos.shape[0]
    fill = lambda v, n: jnp.full((t, n), v, F32)
    return jnp.concatenate([
        cos, cos, fill(1.0, HEAD_DIM - ROT_DIM),
        -sin, fill(0.0, HEAD_DIM - half),
        fill(0.0, half), sin, fill(0.0, HEAD_DIM - ROT_DIM),
        fill(0.0, HEAD_DIM)], axis=1)


def _pack_in_projection(w_in_l, b_in_l, d):
    sizes = (W_HEADS, HEAD_DIM, HEAD_DIM, W_HEADS, HEAD_DIM, N_HEADS, W_HEADS, W_HEADS, W_HEADS, d, d)
    offs = [0]
    for s in sizes:
        offs.append(offs[-1] + s)
    seg = lambda a, i: a[..., offs[i]:offs[i + 1]]
    order = (0, 3, 6, 7, 8, 9, 10, 1, 2, 4, 5)
    pad = SMALL_W - (3 * HEAD_DIM + N_HEADS)
    w_bf = w_in_l.astype(BF16)
    w_big = jnp.concatenate([seg(w_bf, i) for i in order] + [jnp.zeros((d, pad), BF16)], axis=1)
    b_big = jnp.concatenate([seg(b_in_l, i) for i in order] + [jnp.zeros((pad,), b_in_l.dtype)], axis=0)
    return w_big, b_big.reshape(1, -1).astype(F32)


def kernel(x, c, positions, w_ada, b_ada, g_mix, w_in, b_in, qn_a, kn_a, qn_b, kn_b, rel_bias,
           w_oa, w_ob, w_out, g_ffn, w_gu, w_down):
    b, seq, d = x.shape
    depth = w_ada.shape[0]
    t = b * seq
    top_k = min(TOPK_MAX, seq // 4)
    assert seq % KEY_CHUNK == 0 and seq % ROW_TILE == 0 and seq % FFN_ROW_TILE == 0 and seq % BAND_Q == 0 and d % LANES_V7X == 0

    mod = _modulation(c, w_ada, b_ada).reshape(depth, b, 6, d)
    tabs = _rope_tables(positions)
    group = jnp.arange(W_HEADS) // HEAD_DIM
    gmat = jnp.where(group[:, None] == group[None, :], 1.0 / HEAD_DIM, 0.0).astype(BF16)
    lane = jnp.arange(LANES_V7X)
    tile_heads = lambda g: jnp.tile(g.astype(F32), N_HEADS)

    xf = x.reshape(t, d)
    for l in range(depth):
        w_big, b_big = _pack_in_projection(w_in[l], b_in[l], d)
        q_scale = ATTN_SCALE * LOG2_E
        gains = jnp.stack([tile_heads(qn_a[l]) * q_scale, tile_heads(qn_b[l]) * q_scale, tile_heads(kn_b[l])])
        sgains = jnp.stack([
            jnp.concatenate([kn_a[l].astype(F32), jnp.ones((LANES_V7X - HEAD_DIM,), F32)]),
            jnp.where((lane >= HEAD_DIM) & (lane < HEAD_DIM + N_HEADS), INDEX_SCALE, 1.0).astype(F32)])
        qa, qi, qb, kb, vb_t, ga, gb, ka, ki, va_t, w_t = _in_projection(
            xf, mod[l], g_mix[l].reshape(1, d), w_big, b_big, gmat, tabs, gains, sgains, seq)

        to3 = lambda a: a.reshape(b, seq, a.shape[-1])
        attn_a = _sparse_attention(to3(qi), to3(qa), w_t, to3(ki), to3(ka), va_t, top_k)
        pm4 = lambda a: a.reshape(HEAD_PAIRS, b, seq, LANES_V7X)
        attn_b = _band_attention(pm4(qb), pm4(kb), vb_t, _band_bias_t(rel_bias[l]))

        xf = _out_ffn(xf, attn_a.reshape(t, W_HEADS), attn_b.reshape(HEAD_PAIRS, t, LANES_V7X), ga, gb, mod[l],
                      w_oa[l].astype(BF16), w_ob[l].astype(BF16), w_out[l].astype(BF16),
                      g_ffn[l].reshape(1, d), w_gu[l].astype(BF16), w_down[l].astype(BF16), seq)
    return xf.reshape(b, seq, d)
```

```python
import functools

import jax
import jax.numpy as jnp
from jax import lax
from jax.experimental import pallas as pl
from jax.experimental.pallas import tpu as pltpu

F32 = jnp.float32
BF16 = jnp.bfloat16
I32 = jnp.int32

CHUNK = 64
CHUNK_SHIFT = CHUNK.bit_length() - 1
HEAD_DIM = 64
N_HEADS = 8
TOPK_MAX = 256
N_PAST_CHUNKS = 8
REL_LO = -(CHUNK - 1)
REL_HI = 256
ROT_DIM = HEAD_DIM // 4
ROPE_THETA = 500000.0
EPS = 1e-6
W_HEADS = N_HEADS * HEAD_DIM
INDEX_SCALE = (HEAD_DIM ** -0.5) * (N_HEADS ** -0.5)
ATTN_SCALE = HEAD_DIM ** -0.5
LOG2_E = 1.4426950408889634
HEAD_PAIRS = N_HEADS // 2

LANES_V7X = 128
SUBLANES_V7X = 8
VMEM_BYTES_V7X = 64 * 1024 * 1024
VMEM_LIMIT = VMEM_BYTES_V7X * 3 // 4
VMEM_LIMIT_FFN = VMEM_BYTES_V7X * 7 // 8

Q_BLOCK = 128
KEY_CHUNK = 256
SMALL_W = 256
ROW_TILE = 512
FFN_ROW_TILE = 512
INT_MIN = -(2 ** 31)
RANK_LOWEST_FINITE = INT_MIN + (1 << 23)
NEG_BIG = -1e30


def _dot(a, b):
    return jnp.dot(a, b, preferred_element_type=F32)


def _dot_nt(a, b):
    return lax.dot_general(a, b, (((1,), (1,)), ((), ())), preferred_element_type=F32)


def _mod_kernel(c_ref, w_ref, b_ref, o_ref):
    c = c_ref[...]
    c_act = (c * jax.nn.sigmoid(c)).astype(BF16)
    o_ref[0] = _dot(c_act, w_ref[0].astype(BF16)) + b_ref[0]


def _modulation(c, w_ada, b_ada):
    depth, d, n = w_ada.shape
    b = c.shape[0]
    tn = n // 6
    return pl.pallas_call(
        _mod_kernel,
        grid=(depth, n // tn),
        in_specs=[
            pl.BlockSpec((b, d), lambda l, j: (0, 0)),
            pl.BlockSpec((1, d, tn), lambda l, j: (l, 0, j)),
            pl.BlockSpec((1, 1, tn), lambda l, j: (l, 0, j)),
        ],
        out_specs=pl.BlockSpec((1, b, tn), lambda l, j: (l, 0, j)),
        out_shape=jax.ShapeDtypeStruct((depth, b, n), F32),
        name="adaln_modulation",
    )(c, w_ada, b_ada.reshape(depth, 1, n))


def _group_mean_sq(v, gmat):
    return _dot((v * v).astype(BF16), gmat)


def _rope_cols(v, tab_c, tab_s1, tab_s2):
    cols = []
    for c0 in range(0, v.shape[1], LANES_V7X):
        vc = v[:, c0:c0 + LANES_V7X]
        fwd = pltpu.roll(vc, LANES_V7X - ROT_DIM // 2, axis=1)
        bwd = pltpu.roll(vc, ROT_DIM // 2, axis=1)
        cols.append(vc * tab_c + fwd * tab_s1 + bwd * tab_s2)
    return cols[0] if len(cols) == 1 else jnp.concatenate(cols, axis=1)


def _inproj_kernel(x_ref, mod_ref, g_ref, w_ref, b_ref, gmat_ref, tc_ref, ts1_ref, ts2_ref,
                   gain_ref, sgain_ref,
                   qa_ref, qi_ref, qb_ref, kb_ref, vbt_ref, ga_ref, gb_ref, ka_ref, ki_ref, vat_ref, wt_ref):
    d = x_ref.shape[1]
    x = x_ref[...]
    y = x * lax.rsqrt(jnp.mean(x * x, axis=-1, keepdims=True) + EPS) * g_ref[...]
    h = y * (1.0 + mod_ref[0, 1:2, :]) + mod_ref[0, 0:1, :]
    z = _dot(h.astype(BF16), w_ref[...]) + b_ref[...]

    gmat = gmat_ref[...]
    tab_c, tab_s1, tab_s2 = tc_ref[...], ts1_ref[...], ts2_ref[...]
    w = W_HEADS

    def head_norm(v, gain):
        return v * lax.rsqrt(_group_mean_sq(v, gmat) + EPS) * gain

    qa = head_norm(z[:, 0:w], gain_ref[0:1, :])
    qa_ref[...] = _rope_cols(qa, tab_c, tab_s1, tab_s2).astype(BF16)
    qi_ref[...] = _rope_cols(z[:, w:2 * w], tab_c, tab_s1, tab_s2).astype(BF16)
    qb = head_norm(z[:, 2 * w:3 * w], gain_ref[1:2, :]).astype(BF16)
    kb = head_norm(z[:, 3 * w:4 * w], gain_ref[2:3, :]).astype(BF16)
    for pr in range(HEAD_PAIRS):
        cols = slice(pr * LANES_V7X, (pr + 1) * LANES_V7X)
        qb_ref[pr] = qb[:, cols]
        kb_ref[pr] = kb[:, cols]
        for blk in range(vbt_ref.shape[1]):
            rows = slice(blk * BAND_Q, (blk + 1) * BAND_Q)
            vbt_ref[pr, blk] = z[rows, 4 * w + pr * LANES_V7X:4 * w + (pr + 1) * LANES_V7X].T.astype(BF16)
    ga_ref[...] = jax.nn.sigmoid(z[:, 5 * w:5 * w + d]).astype(BF16)
    gb_ref[...] = jax.nn.sigmoid(z[:, 5 * w + d:5 * w + 2 * d]).astype(BF16)

    sm = z[:, 5 * w + 2 * d:5 * w + 2 * d + SMALL_W]
    lane = lax.broadcasted_iota(I32, (1, LANES_V7X), 1)
    first_head = lane < HEAD_DIM
    c0 = sm[:, 0:LANES_V7X]
    c1 = sm[:, LANES_V7X:2 * LANES_V7X]
    ms0 = _group_mean_sq(c0, gmat[0:LANES_V7X, 0:LANES_V7X])
    c0 = jnp.where(first_head, c0 * lax.rsqrt(ms0 + EPS) * sgain_ref[0:1, :], c0)
    c1 = c1 * sgain_ref[1:2, :]
    t_c = jnp.where(first_head, tab_c, 1.0)
    t_s1 = jnp.where(first_head, tab_s1, 0.0)
    t_s2 = jnp.where(first_head, tab_s2, 0.0)
    c0 = _rope_cols(c0, t_c, t_s1, t_s2)
    c1 = _rope_cols(c1, t_c, t_s1, t_s2)
    ka_ref[...] = jnp.concatenate([c0[:, 0:HEAD_DIM]] * 2, axis=1).astype(BF16)
    ki_ref[...] = jnp.concatenate([c1[:, 0:HEAD_DIM]] * 2, axis=1).astype(BF16)
    vat_ref[...] = c0.T[HEAD_DIM:2 * HEAD_DIM, :].astype(BF16)
    w_t = c1.T[HEAD_DIM:HEAD_DIM + N_HEADS, :]
    for blk in range(wt_ref.shape[0]):
        wt_ref[blk] = w_t[:, blk * Q_BLOCK:(blk + 1) * Q_BLOCK]


def _const_spec(shape):
    nd = len(shape)
    return pl.BlockSpec(shape, lambda i: (0,) * nd, pipeline_mode=pl.Buffered(1))


def _in_projection(xf, mod_l, g_mix, w_big, b_big, gmat, tabs, gains, sgains, seq):
    t, d = xf.shape
    tm = ROW_TILE
    tiles_per_batch = seq // tm
    n_big = w_big.shape[1]
    row = lambda width: pl.BlockSpec((tm, width), lambda i: (i, 0))
    pair_major = pl.BlockSpec((HEAD_PAIRS, tm, LANES_V7X), lambda i: (0, i, 0))
    pm_shape = jax.ShapeDtypeStruct((HEAD_PAIRS, t, LANES_V7X), BF16)
    assert tm % BAND_Q == 0 and tm % Q_BLOCK == 0
    vbt_spec = pl.BlockSpec((HEAD_PAIRS, tm // BAND_Q, LANES_V7X, BAND_Q), lambda i: (0, i, 0, 0))
    vbt_shape = jax.ShapeDtypeStruct((HEAD_PAIRS, t // BAND_Q, LANES_V7X, BAND_Q), BF16)
    return pl.pallas_call(
        _inproj_kernel,
        grid=(t // tm,),
        in_specs=[
            row(d),
            pl.BlockSpec((1, 6, d), lambda i: (i // tiles_per_batch, 0, 0)),
            _const_spec((1, d)),
            _const_spec((d, n_big)),
            _const_spec((1, n_big)),
            _const_spec((W_HEADS, W_HEADS)),
            row(LANES_V7X), row(LANES_V7X), row(LANES_V7X),
            _const_spec((3, W_HEADS)),
            _const_spec((2, LANES_V7X)),
        ],
        out_specs=[row(W_HEADS), row(W_HEADS), pair_major, pair_major, vbt_spec, row(d), row(d),
                   row(LANES_V7X), row(LANES_V7X),
                   pl.BlockSpec((HEAD_DIM, tm), lambda i: (0, i)),
                   pl.BlockSpec((tm // Q_BLOCK, N_HEADS, Q_BLOCK), lambda i: (i, 0, 0))],
        out_shape=[jax.ShapeDtypeStruct((t, W_HEADS), BF16)] * 2 + [pm_shape] * 2 + [vbt_shape]
        + [jax.ShapeDtypeStruct((t, d), BF16)] * 2 + [jax.ShapeDtypeStruct((t, LANES_V7X), BF16)] * 2
        + [jax.ShapeDtypeStruct((HEAD_DIM, t), BF16), jax.ShapeDtypeStruct((t // Q_BLOCK, N_HEADS, Q_BLOCK), F32)],
        compiler_params=pltpu.CompilerParams(
            dimension_semantics=("arbitrary",), vmem_limit_bytes=VMEM_LIMIT),
        name="in_projection",
    )(xf, mod_l, g_mix, w_big, b_big, gmat, *tabs, gains, sgains)


def _sparse_kernel(qi_ref, qa_ref, wt_ref, ki_ref, ka_ref, vt_ref, o_ref,
                   qi_s, qa_s, score_s, s_scr, acc_s, tri_s, *, top_k):
    seq = qi_ref.shape[1]
    n_qb = seq // Q_BLOCK
    kc_rows = KEY_CHUNK
    sub_groups = kc_rows // SUBLANES_V7X
    wide = N_HEADS * Q_BLOCK
    r_i = lax.broadcasted_iota(I32, (kc_rows, kc_rows), 0)
    c_i = lax.broadcasted_iota(I32, (kc_rows, kc_rows), 1)
    tri_s[...] = jnp.where(c_i < r_i, 1.0, 0.0).astype(BF16)
    lane128 = lax.broadcasted_iota(I32, (1, LANES_V7X), 1)
    head_lanes = (lane128 < HEAD_DIM, lane128 >= HEAD_DIM)

    def float_of(key):
        return jnp.broadcast_to(
            lax.bitcast_convert_type(key ^ ((key >> 31) & 0x7FFFFFFF), F32), (kc_rows, Q_BLOCK))

    def attend(qb, n_chunks):
        r0 = pl.multiple_of(qb * Q_BLOCK, Q_BLOCK)
        chunks = [j * kc_rows for j in range(n_chunks)]
        w_idx = wt_ref[qb]
        lane = lax.broadcasted_iota(I32, (1, Q_BLOCK), 1)
        limit = (((r0 + lane) >> CHUNK_SHIFT) + 1) * CHUNK
        row_id = lax.broadcasted_iota(I32, (kc_rows, Q_BLOCK), 0)

        for k0 in chunks:
            logits = _dot_nt(ki_ref[0, k0:k0 + kc_rows, :], qi_s[...])
            score = jnp.zeros((kc_rows, Q_BLOCK), F32)
            for h in range(N_HEADS):
                score = score + jnp.maximum(logits[:, h * Q_BLOCK:(h + 1) * Q_BLOCK], 0.0) * w_idx[h:h + 1, :]
            if k0 + kc_rows > (n_chunks - 1) * kc_rows:
                score = jnp.where(row_id < limit - k0, score, -jnp.inf)
            score_s[k0:k0 + kc_rows, :] = score

        def count(pred_fn):
            acc = jnp.zeros((SUBLANES_V7X, Q_BLOCK), I32)
            for k0 in chunks:
                hit = jnp.where(pred_fn(score_s[k0:k0 + kc_rows, :]), 1, 0).astype(I32)
                acc = acc + hit.reshape(sub_groups, SUBLANES_V7X, Q_BLOCK).sum(axis=0)
            return acc.sum(axis=0, keepdims=True)

        def search_bit(i, state):
            thr, cnt_thr = state
            cand = thr + jnp.left_shift(jnp.int32(1), 31 - i)
            cand_f = float_of(cand)
            cnt = count(lambda sc: sc >= cand_f)
            take = cnt >= top_k
            return jnp.where(take, cand, thr), jnp.where(take, cnt, cnt_thr)

        thr0 = jnp.full((1, Q_BLOCK), INT_MIN, I32)
        cnt0 = jnp.full((1, Q_BLOCK), n_chunks * kc_rows, I32)
        need_search = r0 + Q_BLOCK > top_k
        thr, cnt_thr = lax.fori_loop(0, jnp.where(need_search, 32, 0), search_bit, (thr0, cnt0))
        cnt_thr = jnp.where(need_search, cnt_thr, top_k)
        thr = jnp.maximum(thr, RANK_LOWEST_FINITE)
        thr_f = float_of(thr)

        @pl.when(jnp.max(cnt_thr) > top_k)
        def _():
            next_f = float_of(thr + 1)
            need = (top_k - count(lambda sc: sc >= next_f)).astype(F32)
            seen = jnp.zeros((1, Q_BLOCK), F32)
            for k0 in chunks:
                sc = score_s[k0:k0 + kc_rows, :]
                eq = (sc >= thr_f) & (sc < next_f)
                eq_f = jnp.where(eq, 1.0, 0.0)
                rank = _dot(tri_s[...], eq_f.astype(BF16)) + seen
                score_s[k0:k0 + kc_rows, :] = jnp.where(eq & (rank >= need), -jnp.inf, sc)
                seen = seen + jnp.sum(eq_f, axis=0, keepdims=True)

        m_acc = jnp.full((SUBLANES_V7X, wide), -jnp.inf, F32)
        for k0 in chunks:
            s_all = _dot_nt(ka_ref[0, k0:k0 + kc_rows, :], qa_s[...])
            sel = score_s[k0:k0 + kc_rows, :] >= thr_f
            s = jnp.concatenate(
                [jnp.where(sel, s_all[:, h * Q_BLOCK:(h + 1) * Q_BLOCK], -jnp.inf) for h in range(N_HEADS)],
                axis=1)
            s_scr[k0:k0 + kc_rows, :] = s
            m_acc = jnp.maximum(m_acc, s.reshape(sub_groups, SUBLANES_V7X, wide).max(axis=0))
        m_fin = jnp.max(m_acc, axis=0, keepdims=True)

        l_acc = jnp.zeros((SUBLANES_V7X, wide), F32)
        o_t = jnp.zeros((HEAD_DIM, wide), F32)
        for k0 in chunks:
            p = jnp.exp2(s_scr[k0:k0 + kc_rows, :] - m_fin)
            o_t = o_t + _dot(vt_ref[:, k0:k0 + kc_rows], p.astype(BF16))
            l_acc = l_acc + p.reshape(sub_groups, SUBLANES_V7X, wide).sum(axis=0)
        acc_s[...] = o_t / jnp.sum(l_acc, axis=0, keepdims=True)

    def q_block(qb, carry):
        r0 = pl.multiple_of(qb * Q_BLOCK, Q_BLOCK)
        qi = qi_ref[0, pl.ds(r0, Q_BLOCK), :]
        qa = qa_ref[0, pl.ds(r0, Q_BLOCK), :]
        for h in range(N_HEADS):
            cols = slice((h // 2) * LANES_V7X, (h // 2 + 1) * LANES_V7X)
            qi_s[h * Q_BLOCK:(h + 1) * Q_BLOCK, :] = jnp.where(head_lanes[h % 2], qi[:, cols], jnp.zeros_like(qi[:, cols]))
            qa_s[h * Q_BLOCK:(h + 1) * Q_BLOCK, :] = jnp.where(head_lanes[h % 2], qa[:, cols], jnp.zeros_like(qa[:, cols]))
        n_chunks = (r0 + Q_BLOCK + kc_rows - 1) // kc_rows
        for n in range(1, seq // kc_rows + 1):
            pl.when(n_chunks == n)(functools.partial(attend, qb, n))
        o_t = acc_s[...]
        outs = [o_t[:, h * Q_BLOCK:(h + 1) * Q_BLOCK].T for h in range(N_HEADS)]
        o_ref[0, pl.ds(r0, Q_BLOCK), :] = jnp.concatenate(outs, axis=1).astype(BF16)
        return carry

    lax.fori_loop(0, n_qb, q_block, 0)


def _sparse_attention(qi, qa, w_t, ki, ka, v_t, top_k):
    b, seq, w = qi.shape
    batch3 = lambda s1, s2: pl.BlockSpec((1, s1, s2), lambda i: (i, 0, 0))
    return pl.pallas_call(
        functools.partial(_sparse_kernel, top_k=top_k),
        grid=(b,),
        in_specs=[
            batch3(seq, w), batch3(seq, w),
            pl.BlockSpec((seq // Q_BLOCK, N_HEADS, Q_BLOCK), lambda i: (i, 0, 0)),
            batch3(seq, LANES_V7X), batch3(seq, LANES_V7X),
            pl.BlockSpec((HEAD_DIM, seq), lambda i: (0, i)),
        ],
        out_specs=batch3(seq, w),
        out_shape=jax.ShapeDtypeStruct((b, seq, w), BF16),
        scratch_shapes=[
            pltpu.VMEM((N_HEADS * Q_BLOCK, LANES_V7X), BF16),
            pltpu.VMEM((N_HEADS * Q_BLOCK, LANES_V7X), BF16),
            pltpu.VMEM((seq, Q_BLOCK), F32),
            pltpu.VMEM((seq, N_HEADS * Q_BLOCK), F32),
            pltpu.VMEM((HEAD_DIM, N_HEADS * Q_BLOCK), F32),
            pltpu.VMEM((KEY_CHUNK, KEY_CHUNK), BF16),
        ],
        compiler_params=pltpu.CompilerParams(
            dimension_semantics=("arbitrary",), vmem_limit_bytes=VMEM_LIMIT),
        name="sparse_attention",
    )(qi, qa, w_t, ki, ka, v_t)


BAND_Q = 256
BAND_KEYS = N_PAST_CHUNKS * CHUNK + BAND_Q
BAND_PIECES = BAND_KEYS // BAND_Q
BIAS_ROWS_UNROLL = 8


def _band_kernel(q_ref, k_ref, vt_ref, bias_ref, o_ref, s_a, s_b):
    seq = q_ref.shape[2]
    groups = BAND_Q // SUBLANES_V7X
    lane = lax.broadcasted_iota(I32, (1, LANES_V7X), 1)
    head_lanes = (lane < HEAD_DIM, lane >= HEAD_DIM)
    items = [(qb, pr) for qb in range(seq // BAND_Q) for pr in range(HEAD_PAIRS)]

    def pieces_of(qb):
        first = qb - (BAND_PIECES - 1)
        return [(w, first + w) for w in range(BAND_PIECES) if first + w >= 0]

    def logits(item, s_scr):
        qb, pr = item
        q2 = q_ref[pr, 0, qb * BAND_Q:(qb + 1) * BAND_Q, :]
        for e in range(2):
            q_h = jnp.where(head_lanes[e], q2, jnp.zeros_like(q2))
            for w, kb in pieces_of(qb):
                rows = slice(w * BAND_Q, (w + 1) * BAND_Q)
                k2 = k_ref[pr, 0, kb * BAND_Q:(kb + 1) * BAND_Q, :]
                s_scr[e, rows, :] = _dot_nt(k2, q_h) + bias_ref[2 * pr + e, rows, :]

    def softmax_values(item, s_scr):
        qb, pr = item
        pieces = pieces_of(qb)
        o_pair = []
        for e in range(2):
            m_acc = jnp.full((SUBLANES_V7X, BAND_Q), -jnp.inf, F32)
            for w, _ in pieces:
                rows = slice(w * BAND_Q, (w + 1) * BAND_Q)
                m_acc = jnp.maximum(m_acc, s_scr[e, rows, :].reshape(groups, SUBLANES_V7X, BAND_Q).max(axis=0))
            m_fin = jnp.max(m_acc, axis=0, keepdims=True)
            l_acc = jnp.zeros((SUBLANES_V7X, BAND_Q), F32)
            o_t = jnp.zeros((HEAD_DIM, BAND_Q), F32)
            for w, kb in pieces:
                rows = slice(w * BAND_Q, (w + 1) * BAND_Q)
                p = jnp.exp2(s_scr[e, rows, :] - m_fin)
                l_acc = l_acc + p.reshape(groups, SUBLANES_V7X, BAND_Q).sum(axis=0)
                v_h = vt_ref[pr, kb][e * HEAD_DIM:(e + 1) * HEAD_DIM, :]
                o_t = o_t + _dot(v_h, p.astype(BF16))
            o_pair.append(o_t / jnp.sum(l_acc, axis=0, keepdims=True))
        o_ref[pr, 0, qb * BAND_Q:(qb + 1) * BAND_Q, :] = jnp.concatenate(o_pair, axis=0).T.astype(BF16)

    bufs = (s_a, s_b)
    logits(items[0], bufs[0])
    for t in range(1, len(items)):
        logits(items[t], bufs[t % 2])
        softmax_values(items[t - 1], bufs[(t - 1) % 2])
    softmax_values(items[-1], bufs[(len(items) - 1) % 2])


def _band_attention(qb_pm, kb_pm, vb_t, bias_t):
    pairs, b, seq, lanes = qb_pm.shape
    batch4 = pl.BlockSpec((pairs, 1, seq, lanes), lambda i: (0, i, 0, 0))
    return pl.pallas_call(
        _band_kernel,
        grid=(b,),
        in_specs=[batch4, batch4,
                  pl.BlockSpec((pairs, seq // BAND_Q, lanes, BAND_Q), lambda i: (0, i, 0, 0)),
                  _const_spec(bias_t.shape)],
        out_specs=batch4,
        out_shape=jax.ShapeDtypeStruct(qb_pm.shape, BF16),
        scratch_shapes=[pltpu.VMEM((2, BAND_KEYS, BAND_Q), F32)] * 2,
        compiler_params=pltpu.CompilerParams(
            dimension_semantics=("arbitrary",), vmem_limit_bytes=VMEM_LIMIT),
        name="band_attention",
    )(qb_pm, kb_pm, vb_t, bias_t)


def _bias_table_kernel(vec_ref, o_ref):
    period = vec_ref.shape[2]
    base = pltpu.roll(jnp.broadcast_to(vec_ref[0], (SUBLANES_V7X, period)), 0, 1, stride=1, stride_axis=0)
    q_chunk = lax.broadcasted_iota(I32, (SUBLANES_V7X, BAND_Q), 1) >> CHUNK_SHIFT

    def rows(c, carry):
        for u in range(BIAS_ROWS_UNROLL):
            j0 = pl.multiple_of((c * BIAS_ROWS_UNROLL + u) * SUBLANES_V7X, SUBLANES_V7X)
            blk = pltpu.roll(base, j0, 1)[:, :BAND_Q]
            k_chunk = j0 >> CHUNK_SHIFT
            in_band = (q_chunk <= k_chunk) & (q_chunk + N_PAST_CHUNKS >= k_chunk)
            o_ref[0, pl.ds(j0, SUBLANES_V7X), :] = jnp.where(in_band, blk * LOG2_E, NEG_BIG)
        return carry

    lax.fori_loop(0, BAND_KEYS // (SUBLANES_V7X * BIAS_ROWS_UNROLL), rows, 0)


def _band_bias_t(rel_bias_l):
    n_rel = REL_HI - REL_LO + 1
    period = BAND_KEYS + BAND_Q
    pad = N_PAST_CHUNKS * CHUNK
    lo_pos = period - pad + REL_LO
    rb = rel_bias_l.astype(F32)
    hi_val, lo_val = rb[:, n_rel - 1:n_rel], rb[:, 0:1]
    vec = jnp.concatenate([
        jnp.broadcast_to(hi_val, (N_HEADS, BAND_Q + 1)),
        jnp.broadcast_to(lo_val, (N_HEADS, lo_pos - (BAND_Q + 1))),
        rb,
        jnp.broadcast_to(hi_val, (N_HEADS, period - lo_pos - n_rel)),
    ], axis=1)
    assert vec.shape[1] == period and period % LANES_V7X == 0
    return pl.pallas_call(
        _bias_table_kernel,
        grid=(N_HEADS,),
        in_specs=[pl.BlockSpec((1, 1, period), lambda h: (h, 0, 0))],
        out_specs=pl.BlockSpec((1, BAND_KEYS, BAND_Q), lambda h: (h, 0, 0)),
        out_shape=jax.ShapeDtypeStruct((N_HEADS, BAND_KEYS, BAND_Q), F32),
        name="band_bias_table",
    )(vec.reshape(N_HEADS, 1, period))


def _out_ffn_kernel(x_ref, a_ref, b_ref, ga_ref, gb_ref, mod_ref, woa_ref, wob_ref, wout_ref,
                    g_ref, wgu_ref, wdown_ref, o_ref):
    d_ff = wdown_ref.shape[0]
    y_a = _dot(a_ref[...], woa_ref[...])
    attn_b = jnp.concatenate([b_ref[pr] for pr in range(HEAD_PAIRS)], axis=1)
    y_b = _dot(attn_b, wob_ref[...])
    merged = ga_ref[...].astype(F32) * y_a + gb_ref[...].astype(F32) * y_b
    x1 = x_ref[...] + mod_ref[0, 2:3, :] * _dot(merged.astype(BF16), wout_ref[...])

    y = x1 * lax.rsqrt(jnp.mean(x1 * x1, axis=-1, keepdims=True) + EPS) * g_ref[...]
    h2 = y * (1.0 + mod_ref[0, 4:5, :]) + mod_ref[0, 3:4, :]
    gu = _dot(h2.astype(BF16), wgu_ref[...])
    gate, up = gu[:, 0:d_ff], gu[:, d_ff:2 * d_ff]
    act = gate * jax.nn.sigmoid(gate) * up
    o_ref[...] = x1 + mod_ref[0, 5:6, :] * _dot(act.astype(BF16), wdown_ref[...])


def _out_ffn(xf, attn_a, attn_b, ga, gb, mod_l, w_oa, w_ob, w_out, g_ffn, w_gu, w_down, seq):
    t, d = xf.shape
    tm = FFN_ROW_TILE
    tiles_per_batch = seq // tm
    row = lambda width: pl.BlockSpec((tm, width), lambda i: (i, 0))
    return pl.pallas_call(
        _out_ffn_kernel,
        grid=(t // tm,),
        in_specs=[
            row(d), row(W_HEADS), pl.BlockSpec((HEAD_PAIRS, tm, LANES_V7X), lambda i: (0, i, 0)), row(d), row(d),
            pl.BlockSpec((1, 6, d), lambda i: (i // tiles_per_batch, 0, 0)),
            _const_spec(w_oa.shape), _const_spec(w_ob.shape), _const_spec(w_out.shape),
            _const_spec((1, d)), _const_spec(w_gu.shape), _const_spec(w_down.shape),
        ],
        out_specs=row(d),
        out_shape=jax.ShapeDtypeStruct((t, d), F32),
        compiler_params=pltpu.CompilerParams(
            dimension_semantics=("arbitrary",), vmem_limit_bytes=VMEM_LIMIT_FFN),
        name="out_proj_ffn",
    )(xf, attn_a, attn_b, ga, gb, mod_l, w_oa, w_ob, w_out, g_ffn, w_gu, w_down)


def _rope_tables(positions):
    half = ROT_DIM // 2
    inv_freq = ROPE_THETA ** (-jnp.arange(0, ROT_DIM, 2, dtype=F32) / ROT_DIM)
    ang = positions.astype(F32).reshape(-1, 1) * inv_freq
    cos, sin = jnp.cos(ang), jnp.sin(ang)
    t = cos.shape[0]
    ones = jnp.ones((t, HEAD_DIM - ROT_DIM), F32)
    zeros = jnp.zeros((t, HEAD_DIM - half), F32)
    tab_c = jnp.concatenate([cos, cos, ones], axis=1)
    tab_s1 = jnp.concatenate([-sin, zeros], axis=1)
    tab_s2 = jnp.concatenate([jnp.zeros((t, half), F32), sin, zeros[:, half:]], axis=1)
    rep = LANES_V7X // HEAD_DIM
    return tuple(jnp.tile(tb, (1, rep)) for tb in (tab_c, tab_s1, tab_s2))


def _pack_in_projection(w_in_l, b_in_l, d):
    sizes = (W_HEADS, HEAD_DIM, HEAD_DIM, W_HEADS, HEAD_DIM, N_HEADS, W_HEADS, W_HEADS, W_HEADS, d, d)
    offs = [0]
    for s in sizes:
        offs.append(offs[-1] + s)
    seg = lambda a, i: a[..., offs[i]:offs[i + 1]]
    order = (0, 3, 6, 7, 8, 9, 10, 1, 2, 4, 5)
    pad = SMALL_W - (3 * HEAD_DIM + N_HEADS)
    w_bf = w_in_l.astype(BF16)
    w_big = jnp.concatenate([seg(w_bf, i) for i in order] + [jnp.zeros((d, pad), BF16)], axis=1)
    b_big = jnp.concatenate([seg(b_in_l, i) for i in order] + [jnp.zeros((pad,), b_in_l.dtype)], axis=0)
    return w_big, b_big.reshape(1, -1).astype(F32)


def kernel(x, c, positions, w_ada, b_ada, g_mix, w_in, b_in, qn_a, kn_a, qn_b, kn_b, rel_bias,
           w_oa, w_ob, w_out, g_ffn, w_gu, w_down):
    b, seq, d = x.shape
    depth = w_ada.shape[0]
    t = b * seq
    top_k = min(TOPK_MAX, seq // 4)
    assert seq % KEY_CHUNK == 0 and seq % ROW_TILE == 0 and seq % FFN_ROW_TILE == 0 and seq % BAND_Q == 0 and d % LANES_V7X == 0

    mod = _modulation(c, w_ada, b_ada).reshape(depth, b, 6, d)
    tabs = _rope_tables(positions)
    group = jnp.arange(W_HEADS) // HEAD_DIM
    gmat = jnp.where(group[:, None] == group[None, :], 1.0 / HEAD_DIM, 0.0).astype(BF16)
    lane = jnp.arange(LANES_V7X)
    tile_heads = lambda g: jnp.tile(g.astype(F32), N_HEADS)

    xf = x.reshape(t, d)
    for l in range(depth):
        w_big, b_big = _pack_in_projection(w_in[l], b_in[l], d)
        q_scale = ATTN_SCALE * LOG2_E
        gains = jnp.stack([tile_heads(qn_a[l]) * q_scale, tile_heads(qn_b[l]) * q_scale, tile_heads(kn_b[l])])
        sgains = jnp.stack([
            jnp.concatenate([kn_a[l].astype(F32), jnp.ones((LANES_V7X - HEAD_DIM,), F32)]),
            jnp.where((lane >= HEAD_DIM) & (lane < HEAD_DIM + N_HEADS), INDEX_SCALE, 1.0).astype(F32)])
        qa, qi, qb, kb, vb_t, ga, gb, ka, ki, va_t, w_t = _in_projection(
            xf, mod[l], g_mix[l].reshape(1, d), w_big, b_big, gmat, tabs, gains, sgains, seq)

        to3 = lambda a: a.reshape(b, seq, a.shape[-1])
        attn_a = _sparse_attention(to3(qi), to3(qa), w_t, to3(ki), to3(ka), va_t, top_k)
        pm4 = lambda a: a.reshape(HEAD_PAIRS, b, seq, LANES_V7X)
        attn_b = _band_attention(pm4(qb), pm4(kb), vb_t, _band_bias_t(rel_bias[l]))

        xf = _out_ffn(xf, attn_a.reshape(t, W_HEADS), attn_b.reshape(HEAD_PAIRS, t, LANES_V7X), ga, gb, mod[l],
                      w_oa[l].astype(BF16), w_ob[l].astype(BF16), w_out[l].astype(BF16),
                      g_ffn[l].reshape(1, d), w_gu[l].astype(BF16), w_down[l].astype(BF16), seq)
    return xf.reshape(b, seq, d)
```
